```python
import math
import jax, jax.numpy as jnp
from jax import lax
import numpy as np

D_MODEL = 2048
BATCH = 4
SEQ = 2048
DEPTH = 2
DEC_BATCH = 8
DEC_SEQ = 8
PAST_LEN = 16384
PAGE_SIZE = 128

HEAD_DIM = D_MODEL // 16
A_HEADS = 6
A_QK = HEAD_DIM // 2
B_HEADS = 5
C_HEADS = 5
IDX_HEADS = 16
IDX_DIM = HEAD_DIM // 2
TOPK_MAX = 256
D_FF = ((8 * D_MODEL) // 3 + 127) // 128 * 128
ROPE_THETA = 10000.0
NORM_EPS = 1e-6
Q_BLOCK = 128
FORGET_BIAS_MEAN = 3.0
IN_SIZES = (
    A_HEADS * 2 * A_QK, A_HEADS * 2 * A_QK, A_HEADS * HEAD_DIM,
    B_HEADS * HEAD_DIM, B_HEADS * HEAD_DIM, B_HEADS * HEAD_DIM,
    IDX_HEADS * IDX_DIM, IDX_DIM, IDX_HEADS,
    C_HEADS * HEAD_DIM, C_HEADS * HEAD_DIM, C_HEADS * HEAD_DIM, C_HEADS)
D_IN = sum(IN_SIZES)

kernel_name = 'hymba_diff_dsa_fox_macaron_step'


def rms_norm(x, g):
    xf = x.astype(jnp.float32)
    y = xf * lax.rsqrt(jnp.mean(jnp.square(xf), axis=-1, keepdims=True) + NORM_EPS)
    return (y * g.astype(jnp.float32)).astype(x.dtype)


def rope(x, pos):
    half = x.shape[-1] // 2
    inv_freq = ROPE_THETA ** (-jnp.arange(half, dtype=jnp.float32) / half)
    ang = pos.astype(jnp.float32)[:, None] * inv_freq[None, :]
    cos = jnp.cos(ang)[None, :, None, :]
    sin = jnp.sin(ang)[None, :, None, :]
    xf = x.astype(jnp.float32)
    x1, x2 = xf[..., :half], xf[..., half:]
    return jnp.concatenate([x1 * cos - x2 * sin, x2 * cos + x1 * sin], axis=-1).astype(x.dtype)


def swiglu(x, w_gate, w_up, w_down):
    return (jax.nn.silu(x @ w_gate) * (x @ w_up)) @ w_down


def masked_softmax(s, mask):
    return jax.nn.softmax(jnp.where(mask, s, -jnp.inf), axis=-1)


def map_query_blocks(fn, *arrs):
    bsz, tq = arrs[0].shape[:2]
    qb = Q_BLOCK if tq % Q_BLOCK == 0 else tq
    nb = tq // qb
    split = lambda a: jnp.moveaxis(a.reshape(bsz, nb, qb, *a.shape[2:]), 1, 0)
    out = lax.map(lambda blk: fn(*blk), tuple(split(a) for a in arrs))
    return jnp.moveaxis(out, 0, 1).reshape(bsz, tq, *out.shape[3:])


def take_rows(a, idx):
    return jax.vmap(lambda ab, ib: ab[ib])(a, idx)


def gather_past(cache, l, page_table):
    g = cache[l, page_table]
    return g.reshape(page_table.shape[0], -1, *cache.shape[3:])


def paged_take_rows(cache, l, page_table, new, idx):
    past_len = page_table.shape[1] * PAGE_SIZE
    bsz = idx.shape[0]
    ic = jnp.minimum(idx, past_len - 1)
    phys = jnp.take_along_axis(page_table, (ic // PAGE_SIZE).reshape(bsz, -1), axis=1).reshape(idx.shape)
    from_cache = cache[l, phys, ic % PAGE_SIZE]
    from_new = take_rows(new, jnp.clip(idx - past_len, 0, new.shape[1] - 1))
    in_past = (idx < past_len).reshape(idx.shape + (1,) * (new.ndim - 2))
    return jnp.where(in_past, from_cache.astype(new.dtype), from_new)


def differential_attention(q, k, v, q_pos, lam):
    k_pos = jnp.arange(k.shape[1])

    def block(qb, pb):
        s = jnp.einsum('bthcd,bshcd->bhcts', qb, k).astype(jnp.float32) * A_QK ** -0.5
        mask = (k_pos[None, None, :] <= pb[:, :, None])[:, None, None]
        p = masked_softmax(s, mask)
        a = p[:, :, 0] - lam * p[:, :, 1]
        return jnp.einsum('bhts,bshd->bthd', a.astype(v.dtype), v)

    return map_query_blocks(block, q, q_pos)


def forgetting_attention(q, k, v, q_pos, cum_q, cum_k):
    k_pos = jnp.arange(k.shape[1])
    cum_k_t = jnp.moveaxis(cum_k, 1, 2)

    def block(qb, pb, cb):
        s = jnp.einsum('bthd,bshd->bhts', qb, k).astype(jnp.float32) * HEAD_DIM ** -0.5
        s = s + jnp.moveaxis(cb, 1, 2)[..., None] - cum_k_t[:, :, None, :]
        mask = (k_pos[None, None, :] <= pb[:, :, None])[:, None]
        p = masked_softmax(s, mask)
        return jnp.einsum('bhts,bshd->bthd', p.astype(v.dtype), v)

    return map_query_blocks(block, q, q_pos, cum_q)


def sparse_indexed_attention(q, iq, iw, q_pos, ik, topk, gather_kv):
    k_pos = jnp.arange(ik.shape[1])

    def block(qb, iqb, iwb, pb):
        rel = jax.nn.relu(jnp.einsum('bthd,bsd->bths', iqb, ik).astype(jnp.float32))
        score = jnp.einsum('bth,bths->bts', iwb.astype(jnp.float32), rel)
        admissible = k_pos[None, None, :] <= pb[:, :, None]
        _, sel = lax.top_k(jnp.where(admissible, score, -jnp.inf), topk)
        valid = sel <= pb[:, :, None]
        k_sel, v_sel = gather_kv(sel)
        s = jnp.einsum('bthd,btkhd->bhtk', qb, k_sel).astype(jnp.float32) * HEAD_DIM ** -0.5
        p = masked_softmax(s, valid[:, None])
        return jnp.einsum('bhtk,btkhd->bthd', p.astype(v_sel.dtype), v_sel)

    return map_query_blocks(block, q, iq, iw, q_pos)


def token_mixer(h, l, p, pos, past):
    bsz, t, _ = h.shape
    points = [int(c) for c in np.cumsum(IN_SIZES)[:-1]]
    (aq, ak, av, bq, bk, bv, iq, ik, iw, cq, ck, cv, cf) = jnp.split(h @ p['w_in'][l], points, axis=-1)
    aq = rope(aq.reshape(bsz, t, 2 * A_HEADS, A_QK), pos).reshape(bsz, t, A_HEADS, 2, A_QK)
    ak = rope(ak.reshape(bsz, t, 2 * A_HEADS, A_QK), pos).reshape(bsz, t, A_HEADS, 2, A_QK)
    av = av.reshape(bsz, t, A_HEADS, HEAD_DIM)
    bq = rope(bq.reshape(bsz, t, B_HEADS, HEAD_DIM), pos)
    bk = rope(bk.reshape(bsz, t, B_HEADS, HEAD_DIM), pos)
    bv = bv.reshape(bsz, t, B_HEADS, HEAD_DIM)
    iq = rope(iq.reshape(bsz, t, IDX_HEADS, IDX_DIM), pos)
    ik = rope(ik[:, :, None, :], pos)[:, :, 0, :]
    iw = iw * (IDX_HEADS * IDX_DIM) ** -0.5
    cq = cq.reshape(bsz, t, C_HEADS, HEAD_DIM)
    ck = ck.reshape(bsz, t, C_HEADS, HEAD_DIM)
    cv = cv.reshape(bsz, t, C_HEADS, HEAD_DIM)
    logf = jax.nn.log_sigmoid(cf.astype(jnp.float32) + p['b_forget'][l].astype(jnp.float32))

    if past is None:
        a_k_all, a_v_all, ik_all, c_k_all, c_v_all, logf_all = ak, av, ik, ck, cv, logf
        gather_kv = lambda sel: (take_rows(bk, sel), take_rows(bv, sel))
    else:
        pt = past['page_table']
        cat = lambda name, new: jnp.concatenate(
            [gather_past(past[name], l, pt).astype(new.dtype), new], axis=1)
        a_k_all = cat('a_k', ak)
        a_v_all = cat('a_v', av)
        ik_all = cat('b_idx', ik)
        c_k_all = cat('c_k', ck)
        c_v_all = cat('c_v', cv)
        logf_all = cat('c_logf', logf)
        gather_kv = lambda sel: (paged_take_rows(past['b_k'], l, pt, bk, sel),
                                 paged_take_rows(past['b_v'], l, pt, bv, sel))
    s_len = a_k_all.shape[1]
    q_pos = jnp.broadcast_to(pos[None, :], (bsz, t))
    cum = jnp.cumsum(logf_all.astype(jnp.float32), axis=1)

    lam_init = 0.8 - 0.6 * math.exp(-0.3 * l)
    f32 = lambda a: a.astype(jnp.float32)
    lam = (jnp.exp(jnp.sum(f32(p['lambda_q1'][l]) * f32(p['lambda_k1'][l])))
           - jnp.exp(jnp.sum(f32(p['lambda_q2'][l]) * f32(p['lambda_k2'][l]))) + lam_init)
    oa = differential_attention(aq, a_k_all, a_v_all, q_pos, lam)
    oa = rms_norm(oa, p['subln_gain'][l]) * (1.0 - lam_init)
    ob = sparse_indexed_attention(bq, iq, iw, q_pos, ik_all, min(TOPK_MAX, s_len // 4), gather_kv)
    oc = forgetting_attention(cq, c_k_all, c_v_all, q_pos, cum[:, s_len - t:], cum)
    o = jnp.concatenate([oa, ob.astype(oa.dtype), oc.astype(oa.dtype)], axis=2).reshape(bsz, t, D_MODEL)
    return o @ p['w_out'][l], (ak, av, bk, bv, ik, ck, cv, logf)


def decoder_layer(x, l, p, pos, past):
    g = p['norm_gains'][l]
    h = x + 0.5 * rms_norm(swiglu(rms_norm(x, g[0]), p['ffn1_gate'][l], p['ffn1_up'][l], p['ffn1_down'][l]), g[1])
    m, rows = token_mixer(rms_norm(h, g[2]), l, p, pos, past)
    h = h + rms_norm(m, g[3])
    h = h + 0.5 * rms_norm(swiglu(rms_norm(h, g[4]), p['ffn2_gate'][l], p['ffn2_up'][l], p['ffn2_down'][l]), g[5])
    return h, rows


def setup_inputs(seed: int = 0) -> dict:
    key = jax.random.key(seed)
    ks = list(jax.random.split(key, 32))
    nrm = lambda shape, scale=1.0: scale * jax.random.normal(ks.pop(), shape, jnp.float32)
    n_pages = PAST_LEN // PAGE_SIZE
    n_used = DEC_BATCH * n_pages
    n_pool = n_used + max(1, n_used // 4)
    pool = (DEPTH, n_pool, PAGE_SIZE)
    page_table = jax.random.permutation(ks.pop(), n_pool)[:n_used].reshape(DEC_BATCH, n_pages).astype(jnp.int32)
    return {
        'x_prompt': nrm((BATCH, SEQ, D_MODEL)),
        'x_sample': nrm((DEC_BATCH, DEC_SEQ, D_MODEL)),
        'cache_a_k': nrm(pool + (A_HEADS, 2, A_QK)),
        'cache_a_v': nrm(pool + (A_HEADS, HEAD_DIM)),
        'cache_b_k': nrm(pool + (B_HEADS, HEAD_DIM)),
        'cache_b_v': nrm(pool + (B_HEADS, HEAD_DIM)),
        'cache_b_idx': nrm(pool + (IDX_DIM,)),
        'cache_c_k': nrm(pool + (C_HEADS, HEAD_DIM)),
        'cache_c_v': nrm(pool + (C_HEADS, HEAD_DIM)),
        'cache_c_logf': jax.nn.log_sigmoid(nrm(pool + (C_HEADS,)) + FORGET_BIAS_MEAN),
        'page_table': page_table,
        'w_in': nrm((DEPTH, D_MODEL, D_IN), D_MODEL ** -0.5),
        'b_forget': FORGET_BIAS_MEAN + nrm((DEPTH, C_HEADS), 0.5),
        'lambda_q1': nrm((DEPTH, A_QK), 0.1),
        'lambda_k1': nrm((DEPTH, A_QK), 0.1),
        'lambda_q2': nrm((DEPTH, A_QK), 0.1),
        'lambda_k2': nrm((DEPTH, A_QK), 0.1),
        'subln_gain': 1.0 + nrm((DEPTH, HEAD_DIM), 0.02),
        'w_out': nrm((DEPTH, D_MODEL, D_MODEL), D_MODEL ** -0.5),
        'ffn1_gate': nrm((DEPTH, D_MODEL, D_FF), D_MODEL ** -0.5),
        'ffn1_up': nrm((DEPTH, D_MODEL, D_FF), D_MODEL ** -0.5),
        'ffn1_down': nrm((DEPTH, D_FF, D_MODEL), D_FF ** -0.5),
        'ffn2_gate': nrm((DEPTH, D_MODEL, D_FF), D_MODEL ** -0.5),
        'ffn2_up': nrm((DEPTH, D_MODEL, D_FF), D_MODEL ** -0.5),
        'ffn2_down': nrm((DEPTH, D_FF, D_MODEL), D_FF ** -0.5),
        'norm_gains': 1.0 + nrm((DEPTH, 6, D_MODEL), 0.02),
    }


def reference(x_prompt, x_sample, cache_a_k, cache_a_v, cache_b_k, cache_b_v, cache_b_idx,
              cache_c_k, cache_c_v, cache_c_logf, page_table, w_in, b_forget, lambda_q1, lambda_k1,
              lambda_q2, lambda_k2, subln_gain, w_out, ffn1_gate, ffn1_up, ffn1_down,
              ffn2_gate, ffn2_up, ffn2_down, norm_gains):
    p = dict(w_in=w_in, b_forget=b_forget, lambda_q1=lambda_q1, lambda_k1=lambda_k1,
             lambda_q2=lambda_q2, lambda_k2=lambda_k2, subln_gain=subln_gain, w_out=w_out,
             ffn1_gate=ffn1_gate, ffn1_up=ffn1_up, ffn1_down=ffn1_down,
             ffn2_gate=ffn2_gate, ffn2_up=ffn2_up, ffn2_down=ffn2_down, norm_gains=norm_gains)
    past = dict(page_table=page_table, a_k=cache_a_k, a_v=cache_a_v, b_k=cache_b_k, b_v=cache_b_v,
                b_idx=cache_b_idx, c_k=cache_c_k, c_v=cache_c_v, c_logf=cache_c_logf)
    past_len = page_table.shape[1] * PAGE_SIZE
    pos_p = jnp.arange(x_prompt.shape[1])
    pos_s = past_len + jnp.arange(x_sample.shape[1])
    y_prompt, y_sample = x_prompt, x_sample
    rows_p, rows_s = [], []
    for l in range(DEPTH):
        y_prompt, rp = decoder_layer(y_prompt, l, p, pos_p, None)
        y_sample, rs = decoder_layer(y_sample, l, p, pos_s, past)
        rows_p.append(rp)
        rows_s.append(rs)
    (p_a_k, p_a_v, p_b_k, p_b_v, p_b_idx, p_c_k, p_c_v, p_c_logf) = [
        jnp.stack([r[i] for r in rows_p]) for i in range(8)]
    (s_a_k, s_a_v, s_b_k, s_b_v, s_b_idx, s_c_k, s_c_v, s_c_logf) = [
        jnp.stack([r[i] for r in rows_s]) for i in range(8)]
    return (y_prompt, y_sample, p_a_k, p_a_v, p_b_k, p_b_v, p_b_idx, p_c_k, p_c_v, p_c_logf,
            s_a_k, s_a_v, s_b_k, s_b_v, s_b_idx, s_c_k, s_c_v, s_c_logf)
```

```python
import functools
import math

import jax
import jax.numpy as jnp
from jax import lax
from jax.experimental import pallas as pl
from jax.experimental.pallas import tpu as pltpu

F32 = jnp.float32
BF16 = jnp.bfloat16
I32 = jnp.int32

D_MODEL = 2048
HEAD_DIM = 128
A_HEADS = 6
A_QK = 64
B_HEADS = 5
C_HEADS = 5
IDX_HEADS = 16
IDX_DIM = 64
TOPK_MAX = 256
PAGE_SIZE = 128
ROPE_THETA = 10000.0
NORM_EPS = 1e-6
LANES = 128
SUBLANES = 8
MXU_DIM = 256
NEG = -1e30
INT_MIN = -2 ** 31
VMEM_LIMIT = 56 * 1024 * 1024

W_A = A_HEADS * HEAD_DIM
W_B = B_HEADS * HEAD_DIM
W_C = C_HEADS * HEAD_DIM
W_IQ = IDX_HEADS * IDX_DIM
MISC_IW = IDX_DIM
MISC_CF = IDX_DIM + IDX_HEADS


def _cparams(sem):
    return pltpu.CompilerParams(dimension_semantics=sem, vmem_limit_bytes=VMEM_LIMIT)


def _rms(y, g):
    return y * lax.rsqrt(jnp.mean(y * y, axis=-1, keepdims=True) + NORM_EPS) * g


def _dot(a, b):
    return jnp.dot(a, b, preferred_element_type=F32)


def _dot_nt(a, b):
    return lax.dot_general(a, b, (((1,), (1,)), ((), ())), preferred_element_type=F32)


def _ffn_kernel(x_ref, gpre_ref, gpost_ref, gnext_ref, wg_ref, wu_ref, wd_ref,
                h_ref, hn_ref, xn_sc, acc_sc):
    j = pl.program_id(1)

    @pl.when(j == 0)
    def _():
        xn_sc[...] = _rms(x_ref[...], gpre_ref[...]).astype(BF16)
        acc_sc[...] = jnp.zeros_like(acc_sc)

    xn = xn_sc[...]
    g = _dot(xn, wg_ref[...])
    u = _dot(xn, wu_ref[...])
    a = (g * jax.nn.sigmoid(g) * u).astype(BF16)
    acc_sc[...] += _dot(a, wd_ref[...])

    @pl.when(j == pl.num_programs(1) - 1)
    def _():
        h = x_ref[...] + 0.5 * _rms(acc_sc[...], gpost_ref[...])
        h_ref[...] = h
        hn_ref[...] = _rms(h, gnext_ref[...]).astype(BF16)


def _ffn(x, gpre, gpost, gnext, wg, wu, wd, tm, tf):
    m, d = x.shape
    ff = wg.shape[1]
    row = lambda i, j: (i, 0)
    gspec = pl.BlockSpec((1, d), lambda i, j: (0, 0))
    return pl.pallas_call(
        _ffn_kernel,
        grid=(m // tm, ff // tf),
        in_specs=[pl.BlockSpec((tm, d), row), gspec, gspec, gspec,
                  pl.BlockSpec((d, tf), lambda i, j: (0, j)),
                  pl.BlockSpec((d, tf), lambda i, j: (0, j)),
                  pl.BlockSpec((tf, d), lambda i, j: (j, 0))],
        out_specs=[pl.BlockSpec((tm, d), row), pl.BlockSpec((tm, d), row)],
        out_shape=[jax.ShapeDtypeStruct((m, d), F32), jax.ShapeDtypeStruct((m, d), BF16)],
        scratch_shapes=[pltpu.VMEM((tm, d), BF16), pltpu.VMEM((tm, d), F32)],
        compiler_params=_cparams(("parallel", "arbitrary")),
        name="ffn",
    )(x, gpre, gpost, gnext, wg, wu, wd)


def _rope64(y, cos, sin_lo, sin_hi):
    return y * cos + pltpu.roll(y, 96, 1) * sin_lo + pltpu.roll(y, 32, 1) * sin_hi


def _rope128(y, cos, sin_signed):
    return y * cos + pltpu.roll(y, 64, 1) * sin_signed


def _proj_kernel(x_ref, w_ref, tab_ref, *out_refs, pieces):
    y = _dot_nt(x_ref[...], w_ref[...])
    o = 0
    col = 0
    for width, kind, scale, f32_layout, emit_bf16 in pieces:
        for c in range(width // LANES):
            yc = y[:, col + c * LANES:col + (c + 1) * LANES]
            if kind == "rope64":
                yc = _rope64(yc, tab_ref[0], tab_ref[1], tab_ref[2])
            elif kind == "rope128":
                yc = _rope128(yc, tab_ref[3], tab_ref[4])
            if scale != 1.0:
                yc = yc * scale
            k = o
            if f32_layout == "rows":
                out_refs[k][:, c * LANES:(c + 1) * LANES] = yc
            elif f32_layout == "heads":
                out_refs[k][0, c] = yc
            elif f32_layout == "cols":
                out_refs[k][0, c * LANES:(c + 1) * LANES, :] = yc.T
            if f32_layout is not None:
                k += 1
            if emit_bf16:
                out_refs[k][:, c * LANES:(c + 1) * LANES] = yc.astype(BF16)
        o += int(f32_layout is not None) + int(emit_bf16)
        col += width


def _proj(hn, w_t, tab, pieces, tm, tab_blocks, tiles_per_batch):
    m, d = hn.shape
    n = w_t.shape[0]
    bsz = m // (tm * tiles_per_batch)
    t = tm * tiles_per_batch
    rows = lambda i: (i, 0)
    batch_tile = lambda i: (i // tiles_per_batch, 0, i % tiles_per_batch)
    out_specs, out_shape = [], []
    for width, _, _, f32_layout, emit_bf16 in pieces:
        if f32_layout == "rows":
            out_specs.append(pl.BlockSpec((tm, width), rows))
            out_shape.append(jax.ShapeDtypeStruct((m, width), F32))
        elif f32_layout == "heads":
            out_specs.append(pl.BlockSpec((1, width // LANES, tm, LANES), lambda i: batch_tile(i) + (0,)))
            out_shape.append(jax.ShapeDtypeStruct((bsz, width // LANES, t, LANES), F32))
        elif f32_layout == "cols":
            out_specs.append(pl.BlockSpec((1, width, tm), batch_tile))
            out_shape.append(jax.ShapeDtypeStruct((bsz, width, t), F32))
        if emit_bf16:
            out_specs.append(pl.BlockSpec((tm, width), rows))
            out_shape.append(jax.ShapeDtypeStruct((m, width), BF16))
    return pl.pallas_call(
        functools.partial(_proj_kernel, pieces=pieces),
        grid=(m // tm,),
        in_specs=[pl.BlockSpec((tm, d), rows),
                  pl.BlockSpec((n, d), lambda i: (0, 0)),
                  pl.BlockSpec((5, tm, LANES), lambda i: (0, i % tab_blocks, 0))],
        out_specs=out_specs,
        out_shape=out_shape,
        compiler_params=_cparams(("parallel",)),
        name="proj",
    )(hn, w_t, tab)


def _misc_kernel(x_ref, w_ref, tab_ref, bf_ref, misc_ref, misc_t_ref, ikk_ref):
    y = _dot_nt(x_ref[...], w_ref[...])
    lane = lax.broadcasted_iota(I32, y.shape, 1)
    ik = _rope64(y, tab_ref[0], tab_ref[1], tab_ref[2])
    z = y + bf_ref[...]
    logf = jnp.minimum(z, 0.0) - jnp.log1p(jnp.exp(-jnp.abs(z)))
    iw = y * (IDX_HEADS * IDX_DIM) ** -0.5
    misc = jnp.where(lane < MISC_IW, ik, jnp.where(lane < MISC_CF, iw, logf))
    misc_ref[...] = misc
    misc_t_ref[...] = misc.T
    ikk_ref[...] = jnp.where(lane < IDX_DIM, ik, pltpu.roll(ik, IDX_DIM, 1)).astype(BF16)


def _misc(hn, w_t, tab, bf_pad, tm, tab_blocks):
    m, d = hn.shape
    return pl.pallas_call(
        _misc_kernel,
        grid=(m // tm,),
        in_specs=[pl.BlockSpec((tm, d), lambda i: (i, 0)),
                  pl.BlockSpec((LANES, d), lambda i: (0, 0)),
                  pl.BlockSpec((5, tm, LANES), lambda i: (0, i % tab_blocks, 0)),
                  pl.BlockSpec((1, LANES), lambda i: (0, 0))],
        out_specs=[pl.BlockSpec((tm, LANES), lambda i: (i, 0)),
                   pl.BlockSpec((LANES, tm), lambda i: (0, i)),
                   pl.BlockSpec((tm, LANES), lambda i: (i, 0))],
        out_shape=[jax.ShapeDtypeStruct((m, LANES), F32),
                   jax.ShapeDtypeStruct((LANES, m), F32),
                   jax.ShapeDtypeStruct((m, LANES), BF16)],
        compiler_params=_cparams(("parallel",)),
        name="proj_misc",
    )(hn, w_t, tab, bf_pad)


def _outproj_kernel(oa_ref, ob_ref, oc_ref, w_ref, h_ref, g_ref, out_ref):
    m = _dot(oa_ref[...], w_ref[0:W_A, :])
    m += _dot(ob_ref[...], w_ref[W_A:W_A + W_B, :])
    m += _dot(oc_ref[...], w_ref[W_A + W_B:, :])
    out_ref[...] = h_ref[...] + _rms(m, g_ref[...])


def _outproj(oa, ob, oc, w, h, g, tm):
    m, d = h.shape
    row = lambda i: (i, 0)
    return pl.pallas_call(
        _outproj_kernel,
        grid=(m // tm,),
        in_specs=[pl.BlockSpec((tm, W_A), row), pl.BlockSpec((tm, W_B), row),
                  pl.BlockSpec((tm, W_C), row),
                  pl.BlockSpec((d, d), lambda i: (0, 0)),
                  pl.BlockSpec((tm, d), row),
                  pl.BlockSpec((1, d), lambda i: (0, 0))],
        out_specs=pl.BlockSpec((tm, d), row),
        out_shape=jax.ShapeDtypeStruct((m, d), F32),
        compiler_params=_cparams(("parallel",)),
        name="outproj",
    )(oa, ob, oc, w, h, g)


def _split3(x):
    hi = x.astype(BF16)
    r = x - hi.astype(F32)
    mid = r.astype(BF16)
    lo = (r - mid.astype(F32)).astype(BF16)
    return hi, mid, lo


def _tri(n, kind):
    r = lax.broadcasted_iota(I32, (n, n), 0)
    c = lax.broadcasted_iota(I32, (n, n), 1)
    m = {"lower_incl": c <= r, "lower_strict": c < r, "upper_incl": r <= c, "upper_strict": r < c}[kind]
    return jnp.where(m, 1.0, 0.0).astype(BF16)


def _prefix_rows(x, tri):
    hi, mid, lo = _split3(x)
    return _dot(tri, hi) + _dot(tri, mid) + _dot(tri, lo)


def _prefix_lanes(x, tri):
    hi, mid, lo = _split3(x)
    return _dot(hi, tri) + _dot(mid, tri) + _dot(lo, tri)


def _cum_prompt_kernel(misc_ref, col_ref, row_ref, *, blk):
    t = misc_ref.shape[0]
    tri = _tri(blk, "lower_incl")
    carry = jnp.zeros((1, LANES), F32)
    for b in range(t // blk):
        c = _prefix_rows(misc_ref[b * blk:(b + 1) * blk, :], tri) + carry
        col_ref[b * blk:(b + 1) * blk, :] = c
        row_ref[0, b] = c.T
        carry = c[blk - 1:blk, :]


def _cum_prompt(misc, bsz, t, blk):
    return pl.pallas_call(
        functools.partial(_cum_prompt_kernel, blk=blk),
        grid=(bsz,),
        in_specs=[pl.BlockSpec((t, LANES), lambda b: (b, 0))],
        out_specs=[pl.BlockSpec((t, LANES), lambda b: (b, 0)),
                   pl.BlockSpec((1, t // blk, LANES, blk), lambda b: (b, 0, 0, 0))],
        out_shape=[jax.ShapeDtypeStruct((bsz * t, LANES), F32),
                   jax.ShapeDtypeStruct((bsz, t // blk, LANES, blk), F32)],
        compiler_params=_cparams(("parallel",)),
        name="cum_prompt",
    )(misc)


def _sortable(score, admissible=None):
    score = jnp.where(score == 0.0, 0.0, score)
    bits = lax.bitcast_convert_type(score, I32)
    key = jnp.where(bits < 0, bits ^ 0x7FFFFFFF, bits)
    return key if admissible is None else jnp.where(admissible, key, INT_MIN)


def _select_bias(keys, thr, rank, need):
    tie_bias = jnp.where(rank < need, 0.0, NEG)
    bias = jnp.where(keys > thr, 0.0, jnp.where(keys == thr, tie_bias, NEG))
    return jnp.where(keys == INT_MIN, NEG, bias)


def _threshold(count_ge, shape, topk):
    def search(i, thr):
        cand = thr + jnp.left_shift(jnp.int32(1), 31 - i)
        return jnp.where(count_ge(cand) >= topk, cand, thr)
    return lax.fori_loop(0, 32, search, jnp.full(shape, INT_MIN, I32))


def _idx_prompt_kernel(iq_ref, ikk_ref, misc_ref, mask_ref, iqm_sc, keys_sc, *, tq, tk, topk):
    qi = pl.program_id(1)
    nkb = mask_ref.shape[2]
    lane_q = lax.broadcasted_iota(I32, (tq, LANES), 1)
    for h in range(IDX_HEADS):
        blk = iq_ref[:, (h // 2) * LANES:(h // 2 + 1) * LANES]
        keep = (lane_q >= IDX_DIM) if (h % 2) else (lane_q < IDX_DIM)
        iqm_sc[h * tq:(h + 1) * tq, :] = jnp.where(keep, blk, jnp.zeros_like(blk))
    w_t = misc_ref[...].T
    krow = lax.broadcasted_iota(I32, (tk, tq), 0)
    qcol = lax.broadcasted_iota(I32, (tk, tq), 1) + qi * tq

    def score_block(kb, _):
        kblk = ikk_ref[pl.ds(pl.multiple_of(kb * tk, tk), tk), :]
        score = jnp.zeros((tk, tq), F32)
        for h in range(IDX_HEADS):
            s = _dot_nt(kblk, iqm_sc[h * tq:(h + 1) * tq, :])
            score += w_t[MISC_IW + h:MISC_IW + h + 1, :] * jnp.maximum(s, 0.0)
        keys_sc[kb] = _sortable(score, krow + kb * tk <= qcol)
        return 0

    lax.fori_loop(0, qi + 1, score_block, 0)

    def count(pred):
        def body(kb, cnt):
            hit = jnp.where(pred(keys_sc[kb]), 1.0, 0.0)
            return cnt + jnp.sum(hit.reshape(tk // SUBLANES, SUBLANES, tq), axis=0)
        cnt = lax.fori_loop(0, qi + 1, body, jnp.zeros((SUBLANES, tq), F32))
        return jnp.sum(cnt, axis=0, keepdims=True)

    thr = _threshold(lambda cand: count(lambda k: k >= cand), (1, tq), topk)
    need = topk - count(lambda k: k > thr)
    tri = _tri(tk, "lower_strict")

    def emit(kb, seen):
        keys = keys_sc[kb]
        tie_f = jnp.where(keys == thr, 1.0, 0.0)
        rank = _dot(tri, tie_f.astype(BF16)) + seen
        mask_ref[0, 0, kb] = _select_bias(keys, thr, rank, need).T
        return seen + jnp.sum(tie_f, axis=0, keepdims=True)

    lax.fori_loop(0, qi + 1, emit, jnp.zeros((1, tq), F32))

    def fill(kb, _):
        mask_ref[0, 0, kb] = jnp.full((tq, tk), NEG, F32)
        return 0

    lax.fori_loop(qi + 1, nkb, fill, 0)


def _idx_prompt(iq_bf, ikk_bf, misc, bsz, t, tq, tk, topk):
    nq, nkb = t // tq, t // tk
    return pl.pallas_call(
        functools.partial(_idx_prompt_kernel, tq=tq, tk=tk, topk=topk),
        grid=(bsz, nq),
        in_specs=[pl.BlockSpec((tq, W_IQ), lambda b, q: (b * nq + q, 0)),
                  pl.BlockSpec((t, LANES), lambda b, q: (b, 0)),
                  pl.BlockSpec((tq, LANES), lambda b, q: (b * nq + q, 0))],
        out_specs=pl.BlockSpec((1, 1, nkb, tq, tk), lambda b, q: (b, q, 0, 0, 0)),
        out_shape=jax.ShapeDtypeStruct((bsz, nq, nkb, tq, tk), F32),
        scratch_shapes=[pltpu.VMEM((IDX_HEADS * tq, LANES), BF16),
                        pltpu.VMEM((nkb, tk, tq), I32)],
        compiler_params=_cparams(("parallel", "arbitrary")),
        name="idx_prompt",
    )(iq_bf, ikk_bf, misc)


def _lambda(lam_ref, lam_init):
    lq1, lk1, lq2, lk2 = lam_ref[0:1, :], lam_ref[1:2, :], lam_ref[2:3, :], lam_ref[3:4, :]
    return (jnp.exp(jnp.sum(lq1 * lk1, axis=-1, keepdims=True))
            - jnp.exp(jnp.sum(lq2 * lk2, axis=-1, keepdims=True)) + lam_init)


def _attn_prompt_kernel(*refs, mode, tq, tk, lam_init):
    if mode == "a":
        q_ref, k_ref, v_ref, lam_ref, gain_ref, o_ref = refs
    elif mode == "b":
        q_ref, k_ref, v_ref, mask_ref, o_ref = refs
    else:
        q_ref, k_ref, v_ref, cq_ref, ck_ref, o_ref = refs
    qi = pl.program_id(1)
    h = pl.program_id(2)
    q = q_ref[...]
    if mode == "a":
        lane = lax.broadcasted_iota(I32, q.shape, 1)
        zero = jnp.zeros_like(q)
        qs = jnp.concatenate([jnp.where(lane < A_QK, q, zero), jnp.where(lane >= A_QK, q, zero)], axis=0)
        scale = None
    else:
        qs = q
        scale = HEAD_DIM ** -0.5
    rows = qs.shape[0]
    if mode == "c":
        lane = lax.broadcasted_iota(I32, (tq, LANES), 1)
        cum_q = jnp.sum(jnp.where(lane == MISC_CF + h, cq_ref[...], 0.0), axis=-1, keepdims=True)

    def step(kb, carry, diagonal):
        m, l, acc = carry
        off = pl.multiple_of(kb * tk, tk)
        s = _dot_nt(qs, k_ref[pl.ds(off, tk), :])
        if scale is not None:
            s = s * scale
        if mode == "b":
            s = s + mask_ref[0, 0, kb]
        if mode == "c":
            s = s + (cum_q - ck_ref[0, kb, pl.ds(h, 1), :])
        if diagonal:
            r = lax.broadcasted_iota(I32, (rows, tk), 0)
            c = lax.broadcasted_iota(I32, (rows, tk), 1)
            if rows > tq:
                r = jnp.where(r >= tq, r - tq, r)
            s = jnp.where(c <= r, s, NEG)
        m_new = jnp.maximum(m, jnp.max(s, axis=-1, keepdims=True))
        alpha = jnp.exp(m - m_new)
        p = jnp.exp(s - m_new)
        l = alpha * l + jnp.sum(p, axis=-1, keepdims=True)
        acc = alpha * acc + _dot(p.astype(BF16), v_ref[pl.ds(off, tk), :])
        return m_new, l, acc

    carry = (jnp.full((rows, 1), NEG, F32), jnp.zeros((rows, 1), F32), jnp.zeros((rows, LANES), F32))
    carry = lax.fori_loop(0, qi, lambda kb, c: step(kb, c, False), carry)
    _, l, acc = step(qi, carry, True)
    o = acc / l
    if mode == "a":
        lam = _lambda(lam_ref, lam_init)
        o = o[:tq] - lam * o[tq:]
        o = _rms(o, gain_ref[...]) * (1.0 - lam_init)
    o_ref[...] = o.astype(BF16)


def _attn_prompt(mode, q, k, v, extra, bsz, t, tq, lam_init=0.0):
    heads = q.shape[1] // LANES
    nq = t // tq
    tk = tq
    qspec = pl.BlockSpec((tq, LANES), lambda b, i, h: (b * nq + i, h))
    kvspec = pl.BlockSpec((t, LANES), lambda b, i, h: (b, h))
    if mode == "a":
        especs = [pl.BlockSpec((4, A_QK), lambda b, i, h: (0, 0)),
                  pl.BlockSpec((1, LANES), lambda b, i, h: (0, 0))]
    elif mode == "b":
        especs = [pl.BlockSpec((1, 1, t // tk, tq, tk), lambda b, i, h: (b, i, 0, 0, 0))]
    else:
        especs = [pl.BlockSpec((tq, LANES), lambda b, i, h: (b * nq + i, 0)),
                  pl.BlockSpec((1, t // tk, SUBLANES, tk), lambda b, i, h: (b, 0, MISC_CF // SUBLANES, 0))]
    return pl.pallas_call(
        functools.partial(_attn_prompt_kernel, mode=mode, tq=tq, tk=tk, lam_init=lam_init),
        grid=(bsz, nq, heads),
        in_specs=[qspec, kvspec, kvspec] + especs,
        out_specs=qspec,
        out_shape=jax.ShapeDtypeStruct(q.shape, BF16),
        compiler_params=_cparams(("parallel", "parallel", "arbitrary")),
        name="attn_prompt_" + mode,
    )(q, k, v, *extra)


def _page_specs(shape, group):
    def spec(g):
        return pl.BlockSpec((1,) + shape, lambda b, s, pt: (pt[b, s * group + g],) + (0,) * len(shape))
    return [spec(g) for g in range(group)]


def _exclusive_page_scan(tot, before_sc):
    seen = jnp.zeros((SUBLANES, 1), F32)
    for p in range(tot.shape[0]):
        before_sc[p] = seen
        seen = seen + tot[p]
    return seen


def _cum_sample_kernel(pt_ref, *refs, group):
    del pt_ref
    page_refs = refs[:group]
    new_ref, cumk_ref, cumq_ref, lf_sc, before_sc = refs[group:]
    s = pl.program_id(1)
    npg = lf_sc.shape[0]
    for g in range(group):
        lf_sc[s * group + g] = page_refs[g][0]

    @pl.when(s == pl.num_programs(1) - 1)
    def _():
        tri = _tri(LANES, "upper_incl")
        c = _prefix_lanes(lf_sc[...].reshape(npg * SUBLANES, LANES), tri).reshape(npg, SUBLANES, LANES)
        total = _exclusive_page_scan(c[:, :, LANES - 1:LANES], before_sc)
        cumk_ref[0, 0:npg] = c + before_sc[...]
        pad = jnp.zeros((LANES - SUBLANES, LANES), F32)
        lf_new = jnp.concatenate([new_ref[0], pad], axis=0).T[MISC_CF:MISC_CF + SUBLANES, :]
        c_new = _prefix_lanes(lf_new, tri) + total
        cumk_ref[0, npg] = c_new
        cumq_ref[0] = jnp.concatenate([c_new, pad], axis=0).T[0:SUBLANES, :]


def _cum_sample(logf_pages, misc_new, page_table, group):
    bsz, npg = page_table.shape
    grid_spec = pltpu.PrefetchScalarGridSpec(
        num_scalar_prefetch=1,
        grid=(bsz, npg // group),
        in_specs=_page_specs((SUBLANES, PAGE_SIZE), group)
        + [pl.BlockSpec((1, SUBLANES, LANES), lambda b, s, pt: (b, 0, 0))],
        out_specs=[pl.BlockSpec((1, npg + 1, SUBLANES, PAGE_SIZE), lambda b, s, pt: (b, 0, 0, 0)),
                   pl.BlockSpec((1, SUBLANES, LANES), lambda b, s, pt: (b, 0, 0))],
        scratch_shapes=[pltpu.VMEM((npg, SUBLANES, PAGE_SIZE), F32),
                        pltpu.VMEM((npg, SUBLANES, 1), F32)],
    )
    return pl.pallas_call(
        functools.partial(_cum_sample_kernel, group=group),
        grid_spec=grid_spec,
        out_shape=[jax.ShapeDtypeStruct((bsz, npg + 1, SUBLANES, PAGE_SIZE), F32),
                   jax.ShapeDtypeStruct((bsz, SUBLANES, LANES), F32)],
        compiler_params=_cparams(("parallel", "arbitrary")),
        name="cum_sample",
    )(page_table, *([logf_pages] * group), misc_new)


def _idx_sample_kernel(pt_ref, *refs, group, topk):
    del pt_ref
    page_refs = refs[:group]
    iq_ref, iw_ref, iknew_ref, mask_ref, keys_sc, before_sc = refs[group:]
    s = pl.program_id(1)
    npg = keys_sc.shape[0] - 1
    iq = iq_ref[0]
    w = jnp.broadcast_to(iw_ref[0], (IDX_HEADS * SUBLANES, LANES))

    def scores(qk):
        rel = jnp.maximum(qk, 0.0) * w
        return jnp.sum(rel.reshape(IDX_HEADS, SUBLANES, LANES), axis=0)

    for g in range(group):
        keys_sc[s * group + g] = _sortable(scores(_dot(iq, page_refs[g][0].astype(BF16))))

    @pl.when(s == pl.num_programs(1) - 1)
    def _():
        tok = lax.broadcasted_iota(I32, (SUBLANES, LANES), 0)
        lane = lax.broadcasted_iota(I32, (SUBLANES, LANES), 1)
        keys_sc[npg] = _sortable(scores(_dot_nt(iq, iknew_ref[0].astype(BF16))), lane <= tok)
        keys = keys_sc[...]

        def count(pred):
            hit = jnp.sum(jnp.where(pred, 1.0, 0.0), axis=0)
            return jnp.sum(hit, axis=-1, keepdims=True)

        thr = _threshold(lambda cand: count(keys >= cand[None]), (SUBLANES, 1), topk)
        need = topk - count(keys > thr[None])
        tie_f = jnp.where(keys == thr[None], 1.0, 0.0)
        rank = _dot(tie_f.astype(BF16).reshape((npg + 1) * SUBLANES, LANES), _tri(LANES, "upper_strict"))
        _exclusive_page_scan(jnp.sum(tie_f, axis=-1, keepdims=True), before_sc)
        rank = rank.reshape(npg + 1, SUBLANES, LANES) + before_sc[...]
        mask_ref[0] = _select_bias(keys, thr[None], rank, need[None])


def _idx_sample(idx_pages, iq_rows, iw_rows, ik_new, page_table, group, topk):
    bsz, npg = page_table.shape
    grid_spec = pltpu.PrefetchScalarGridSpec(
        num_scalar_prefetch=1,
        grid=(bsz, npg // group),
        in_specs=_page_specs((IDX_DIM, PAGE_SIZE), group)
        + [pl.BlockSpec((1, IDX_HEADS * SUBLANES, IDX_DIM), lambda b, s, pt: (b, 0, 0)),
           pl.BlockSpec((1, IDX_HEADS * SUBLANES, 1), lambda b, s, pt: (b, 0, 0)),
           pl.BlockSpec((1, PAGE_SIZE, IDX_DIM), lambda b, s, pt: (b, 0, 0))],
        out_specs=pl.BlockSpec((1, npg + 1, SUBLANES, PAGE_SIZE), lambda b, s, pt: (b, 0, 0, 0)),
        scratch_shapes=[pltpu.VMEM((npg + 1, SUBLANES, PAGE_SIZE), I32),
                        pltpu.VMEM((npg + 1, SUBLANES, 1), F32)],
    )
    return pl.pallas_call(
        functools.partial(_idx_sample_kernel, group=group, topk=topk),
        grid_spec=grid_spec,
        out_shape=jax.ShapeDtypeStruct((bsz, npg + 1, SUBLANES, PAGE_SIZE), F32),
        compiler_params=_cparams(("parallel", "arbitrary")),
        name="idx_sample",
    )(page_table, *([idx_pages] * group), iq_rows, iw_rows, ik_new)


def _attn_sample_kernel(pt_ref, *refs, mode, group, npg, rows, lam_init):
    del pt_ref
    k_pages, v_pages = refs[:group], refs[group:2 * group]
    rest = refs[2 * group:]
    if mode == "a":
        q_ref, knew_ref, vnew_ref, lam_ref, gain_ref, o_ref, qbd_sc, m_sc, l_sc, acc_sc = rest
    elif mode == "b":
        q_ref, knew_ref, vnew_ref, mask_ref, o_ref, qbd_sc, m_sc, l_sc, acc_sc = rest
    else:
        q_ref, knew_ref, vnew_ref, cumk_ref, cumq_ref, o_ref, qbd_sc, m_sc, l_sc, acc_sc = rest
    s = pl.program_id(1)
    width = q_ref.shape[-1]
    map_width = A_QK if mode == "a" else HEAD_DIM
    heads = width // HEAD_DIM
    reps = rows // SUBLANES
    scale = None if mode == "a" else HEAD_DIM ** -0.5

    def bias(page, new):
        tok = lax.broadcasted_iota(I32, (rows, LANES), 0) % SUBLANES
        lane = lax.broadcasted_iota(I32, (rows, LANES), 1)
        causal = jnp.where(lane <= tok, 0.0, NEG)
        if mode == "a":
            return causal if new else None
        if mode == "b":
            return jnp.concatenate([mask_ref[0, page]] * reps, axis=0)
        ck = cumk_ref[0, page]
        cq = cumq_ref[0]
        parts = [cq[:, hh:hh + 1] - ck[hh:hh + 1, :] for hh in range(heads)]
        parts += [jnp.zeros((SUBLANES, LANES), F32)] * (reps - heads)
        b = jnp.concatenate(parts, axis=0)
        return b + causal if new else b

    def update(tiles):
        m_old = m_sc[...]
        m_new = m_old
        for sc, _ in tiles:
            m_new = jnp.maximum(m_new, jnp.max(sc, axis=-1, keepdims=True))
        alpha = jnp.exp(m_old - m_new)
        l = alpha * l_sc[...]
        acc = alpha * acc_sc[...]
        for sc, v in tiles:
            p = jnp.exp(sc - m_new)
            l = l + jnp.sum(p, axis=-1, keepdims=True)
            acc = acc + _dot(p.astype(BF16), v)
        m_sc[...] = m_new
        l_sc[...] = l
        acc_sc[...] = acc

    def by_head(ref):
        return jnp.concatenate([ref[0, hh] for hh in range(heads)], axis=1).astype(BF16)

    def tile(sc, v, page, new):
        if scale is not None:
            sc = sc * scale
        b = bias(page, new)
        if b is not None:
            sc = sc + b
        return sc, v

    def cached_tile(g):
        q = qbd_sc[...]
        if mode == "a":
            sc = _dot(q, k_pages[g][0].astype(BF16))
        else:
            sc = _dot_nt(q, by_head(k_pages[g]))
        return tile(sc, by_head(v_pages[g]), s * group + g, False)

    def new_tile():
        sc = _dot_nt(qbd_sc[...], knew_ref[0].astype(BF16))
        return tile(sc, vnew_ref[0].astype(BF16), npg, True)

    @pl.when(s == 0)
    def _():
        q = jnp.concatenate([q_ref[0]] * reps, axis=0)
        r = lax.broadcasted_iota(I32, q.shape, 0) // SUBLANES
        c = lax.broadcasted_iota(I32, q.shape, 1) // map_width
        qbd_sc[...] = jnp.where(r == c, q, 0.0).astype(BF16)
        m_sc[...] = jnp.full(m_sc.shape, NEG, F32)
        l_sc[...] = jnp.zeros_like(l_sc)
        acc_sc[...] = jnp.zeros_like(acc_sc)
        update([new_tile()])

    update([cached_tile(g) for g in range(group)])

    @pl.when(s == npg // group - 1)
    def _():
        on = acc_sc[...] / l_sc[...]
        lam = _lambda(lam_ref, lam_init) if mode == "a" else None
        for hh in range(heads):
            cols = slice(hh * HEAD_DIM, (hh + 1) * HEAD_DIM)
            if mode == "a":
                o = (on[2 * hh * SUBLANES:(2 * hh + 1) * SUBLANES, cols]
                     - lam * on[(2 * hh + 1) * SUBLANES:(2 * hh + 2) * SUBLANES, cols])
                o = _rms(o, gain_ref[...]) * (1.0 - lam_init)
            else:
                o = on[hh * SUBLANES:(hh + 1) * SUBLANES, cols]
            o_ref[0, :, cols] = o


def _attn_sample(mode, q, k_cache, v_cache, k_new, v_new, extra, page_table, group, lam_init=0.0):
    bsz, npg = page_table.shape
    width = q.shape[-1]
    rows = 2 * A_HEADS * SUBLANES if mode == "a" else 48
    batch3 = lambda b, s, pt: (b, 0, 0)
    new_spec = pl.BlockSpec((1, PAGE_SIZE, width), batch3)
    if mode == "a":
        especs = [pl.BlockSpec((4, A_QK), lambda b, s, pt: (0, 0)),
                  pl.BlockSpec((1, LANES), lambda b, s, pt: (0, 0))]
    elif mode == "b":
        especs = [pl.BlockSpec((1, npg + 1, SUBLANES, PAGE_SIZE), lambda b, s, pt: (b, 0, 0, 0))]
    else:
        especs = [pl.BlockSpec((1, npg + 1, SUBLANES, PAGE_SIZE), lambda b, s, pt: (b, 0, 0, 0)),
                  pl.BlockSpec((1, SUBLANES, LANES), batch3)]
    grid_spec = pltpu.PrefetchScalarGridSpec(
        num_scalar_prefetch=1,
        grid=(bsz, npg // group),
        in_specs=_page_specs(k_cache.shape[1:], group) + _page_specs(v_cache.shape[1:], group)
        + [pl.BlockSpec((1, SUBLANES, width), batch3), new_spec, new_spec] + especs,
        out_specs=pl.BlockSpec((1, SUBLANES, width), batch3),
        scratch_shapes=[pltpu.VMEM((rows, width), BF16), pltpu.VMEM((rows, 1), F32),
                        pltpu.VMEM((rows, 1), F32), pltpu.VMEM((rows, width), F32)],
    )
    return pl.pallas_call(
        functools.partial(_attn_sample_kernel, mode=mode, group=group, npg=npg, rows=rows,
                          lam_init=lam_init),
        grid_spec=grid_spec,
        out_shape=jax.ShapeDtypeStruct((bsz, SUBLANES, width), F32),
        compiler_params=_cparams(("parallel", "arbitrary")),
        name="attn_sample_" + mode,
    )(page_table, *([k_cache] * group), *([v_cache] * group), q, k_new, v_new, *extra)


_IN_SIZES = (W_A, W_A, W_A, W_B, W_B, W_B, W_IQ, IDX_DIM, IDX_HEADS, W_C, W_C, W_C, C_HEADS)
_IN_NAMES = ("aq", "ak", "av", "bq", "bk", "bv", "iq", "ik", "iw", "cq", "ck", "cv", "cf")

_PROJ_GROUPS = (
    (("aq", "ak", "iq"),
     ((W_A, "rope64", A_QK ** -0.5, None, True), (W_A, "rope64", 1.0, "cols", True),
      (W_IQ, "rope64", 1.0, None, True))),
    (("bq", "bk"),
     ((W_B, "rope128", 1.0, None, True), (W_B, "rope128", 1.0, "heads", True))),
    (("av", "bv", "cq", "ck", "cv"),
     ((W_A, "id", 1.0, "heads", True), (W_B, "id", 1.0, "heads", True), (W_C, "id", 1.0, None, True),
      (W_C, "id", 1.0, "heads", True), (W_C, "id", 1.0, "heads", True))),
)


def _row_major(pieces):
    return tuple((w, kind, scale, None if lay is None else "rows", bf) for w, kind, scale, lay, bf in pieces)


def _rope_tables(pos):
    lane = jnp.arange(LANES)
    posf = pos.astype(F32)[:, None]

    def cs(half):
        inv_freq = ROPE_THETA ** (-jnp.arange(half, dtype=F32) / half)
        ang = posf * inv_freq[lane % half][None, :]
        return jnp.cos(ang), jnp.sin(ang)

    c64, s64 = cs(IDX_DIM // 2)
    first64 = (lane % IDX_DIM) < IDX_DIM // 2
    c128, s128 = cs(HEAD_DIM // 2)
    return jnp.stack([c64, jnp.where(first64, -s64, 0.0), jnp.where(first64, 0.0, s64),
                      c128, jnp.where(lane < HEAD_DIM // 2, -s128, s128)]).astype(F32)


def _pad_cols(w, n):
    return jnp.pad(w, ((0, 0), (0, n - w.shape[1])))


def _layer_weights(l, w_in, b_forget, lam_q1, lam_k1, lam_q2, lam_k2, subln_gain, w_out,
                   f1g, f1u, f1d, f2g, f2u, f2d, norm_gains, ff_pad):
    offs = {}
    o = 0
    for name, size in zip(_IN_NAMES, _IN_SIZES):
        offs[name] = (o, o + size)
        o += size
    w_t = jnp.transpose(w_in, (2, 0, 1))
    cols = lambda names: jnp.concatenate([w_t[offs[n][0]:offs[n][1], l, :] for n in names], axis=0).astype(BF16)
    ff = f1g.shape[-1]
    up = lambda w: _pad_cols(w[l].astype(BF16), ff_pad)
    down = lambda w: jnp.pad(w[l].astype(BF16), ((0, ff_pad - ff), (0, 0)))
    misc_t = cols(("ik", "iw", "cf"))
    return dict(
        groups=[cols(names) for names, _ in _PROJ_GROUPS],
        misc=jnp.pad(misc_t, ((0, LANES - misc_t.shape[0]), (0, 0))),
        bf=jnp.zeros((1, LANES), F32).at[0, MISC_CF:MISC_CF + C_HEADS].set(b_forget[l].astype(F32)),
        lam=jnp.stack([lam_q1[l], lam_k1[l], lam_q2[l], lam_k2[l]]).astype(F32),
        gain=subln_gain[l].astype(F32)[None, :],
        w_out=w_out[l].astype(BF16),
        ffn1=(up(f1g), up(f1u), down(f1d)),
        ffn2=(up(f2g), up(f2u), down(f2d)),
        g=[norm_gains[l, i].astype(F32)[None, :] for i in range(6)],
    )


def _project(hn, wts, tab, tm, tab_blocks, native):
    out = {}
    for (names, pieces), w in zip(_PROJ_GROUPS, wts["groups"]):
        pieces = pieces if native else _row_major(pieces)
        res = _proj(hn, w, tab, pieces, tm, tab_blocks, tab_blocks if native else 1)
        k = 0
        for name, (_, _, _, f32_layout, emit_bf16) in zip(names, pieces):
            if f32_layout is not None:
                out[name] = res[k]
                k += 1
            if emit_bf16:
                out[name + "_bf"] = res[k]
                k += 1
    out["misc"], out["misc_t"], out["ikk_bf"] = _misc(hn, wts["misc"], tab, wts["bf"], tm, tab_blocks)
    return out


def _rows_out_native(pr, bsz, seq):
    heads = lambda a: jnp.transpose(a, (0, 2, 1, 3))
    misc_t = pr["misc_t"]
    lanes = lambda lo, n: jnp.transpose(misc_t[lo:lo + n].reshape(n, bsz, seq), (1, 2, 0))
    ak = jnp.transpose(pr["ak"].reshape(bsz, A_HEADS, 2, A_QK, seq), (0, 4, 1, 2, 3))
    return (ak, heads(pr["av"]), heads(pr["bk"]), heads(pr["bv"]), lanes(0, IDX_DIM),
            heads(pr["ck"]), heads(pr["cv"]), lanes(MISC_CF, C_HEADS))


def _rows_out(pr, lead):
    misc = pr["misc"]
    return (pr["ak"].reshape(lead + (A_HEADS, 2, A_QK)), pr["av"].reshape(lead + (A_HEADS, HEAD_DIM)),
            pr["bk"].reshape(lead + (B_HEADS, HEAD_DIM)), pr["bv"].reshape(lead + (B_HEADS, HEAD_DIM)),
            misc[:, :IDX_DIM].reshape(lead + (IDX_DIM,)),
            pr["ck"].reshape(lead + (C_HEADS, HEAD_DIM)), pr["cv"].reshape(lead + (C_HEADS, HEAD_DIM)),
            misc[:, MISC_CF:MISC_CF + C_HEADS].reshape(lead + (C_HEADS,)))


def kernel(x_prompt, x_sample, cache_a_k, cache_a_v, cache_b_k, cache_b_v, cache_b_idx,
           cache_c_k, cache_c_v, cache_c_logf, page_table, w_in, b_forget, lambda_q1, lambda_k1,
           lambda_q2, lambda_k2, subln_gain, w_out, ffn1_gate, ffn1_up, ffn1_down,
           ffn2_gate, ffn2_up, ffn2_down, norm_gains):
    bsz, seq, d = x_prompt.shape
    dbs, dseq, _ = x_sample.shape
    depth = w_in.shape[0]
    n_pool = cache_a_k.shape[1]
    npg = page_table.shape[1]
    past_len = npg * PAGE_SIZE
    ff = ffn1_gate.shape[-1]
    tf = 2 * MXU_DIM
    ff_pad = -(-ff // tf) * tf
    tm_p, tm_proj, tq = 512, 256, 256
    ms = dbs * dseq
    group = 8
    assert dseq == SUBLANES and seq % tm_p == 0 and npg % group == 0

    tab_p = _rope_tables(jnp.arange(seq))
    tab_s = _rope_tables(past_len + (jnp.arange(ms) % dseq))
    topk_p = min(TOPK_MAX, seq // 4)
    topk_s = min(TOPK_MAX, (past_len + dseq) // 4)

    xp = x_prompt.reshape(bsz * seq, d)
    xs = x_sample.reshape(ms, d)
    rows_p, rows_s = [], []
    pad_new = lambda a: jnp.pad(a.reshape(dbs, dseq, -1), ((0, 0), (0, PAGE_SIZE - dseq), (0, 0)))
    for l in range(depth):
        wts = _layer_weights(l, w_in, b_forget, lambda_q1, lambda_k1, lambda_q2, lambda_k2, subln_gain,
                             w_out, ffn1_gate, ffn1_up, ffn1_down, ffn2_gate, ffn2_up, ffn2_down,
                             norm_gains, ff_pad)
        g = wts["g"]
        lam_init = 0.8 - 0.6 * math.exp(-0.3 * l)

        h1, hn = _ffn(xp, g[0], g[1], g[2], *wts["ffn1"], tm_p, tf)
        pr = _project(hn, wts, tab_p, tm_proj, seq // tm_proj, True)
        cum_col, cum_row = _cum_prompt(pr["misc"], bsz, seq, tq)
        mask = _idx_prompt(pr["iq_bf"], pr["ikk_bf"], pr["misc"], bsz, seq, tq, tq, topk_p)
        oa = _attn_prompt("a", pr["aq_bf"], pr["ak_bf"], pr["av_bf"], (wts["lam"], wts["gain"]),
                          bsz, seq, tq, lam_init)
        ob = _attn_prompt("b", pr["bq_bf"], pr["bk_bf"], pr["bv_bf"], (mask,), bsz, seq, tq)
        oc = _attn_prompt("c", pr["cq_bf"], pr["ck_bf"], pr["cv_bf"], (cum_col, cum_row), bsz, seq, tq)
        h2 = _outproj(oa, ob, oc, wts["w_out"], h1, g[3], tm_p)
        xp, _ = _ffn(h2, g[4], g[5], g[5], *wts["ffn2"], tm_p, tf)
        rows_p.append(_rows_out_native(pr, bsz, seq))

        h1, hn = _ffn(xs, g[0], g[1], g[2], *wts["ffn1"], ms, tf)
        pr = _project(hn, wts, tab_s, ms, 1, False)
        misc3 = pr["misc"].reshape(dbs, dseq, LANES)
        logf_pages = jnp.pad(jnp.swapaxes(cache_c_logf[l], 1, 2).astype(F32),
                             ((0, 0), (0, SUBLANES - C_HEADS), (0, 0)))
        cum_k, cum_q = _cum_sample(logf_pages, misc3, page_table, 16)
        iq_rows = jnp.swapaxes(pr["iq_bf"].reshape(dbs, dseq, IDX_HEADS, IDX_DIM), 1, 2)
        iq_rows = iq_rows.reshape(dbs, IDX_HEADS * dseq, IDX_DIM)
        iw_rows = jnp.swapaxes(misc3[:, :, MISC_IW:MISC_CF], 1, 2).reshape(dbs, IDX_HEADS * dseq, 1)
        pt_l = page_table + l * n_pool
        pool = depth * n_pool
        kt_pages = lambda c, w: jnp.moveaxis(c.reshape(pool, PAGE_SIZE, w), 1, 2)
        head_pages = lambda c: jnp.swapaxes(c.reshape((pool,) + c.shape[2:]), 1, 2)
        mask = _idx_sample(kt_pages(cache_b_idx, IDX_DIM), iq_rows, iw_rows, pad_new(pr["misc"][:, :IDX_DIM]),
                           pt_l, 16, topk_s)
        q3 = lambda name: pr[name + "_bf"].astype(F32).reshape(dbs, dseq, -1)
        oa = _attn_sample("a", q3("aq"), kt_pages(cache_a_k, W_A), head_pages(cache_a_v), pad_new(pr["ak"]),
                          pad_new(pr["av"]), (wts["lam"], wts["gain"]), pt_l, group, lam_init)
        ob = _attn_sample("b", q3("bq"), head_pages(cache_b_k), head_pages(cache_b_v), pad_new(pr["bk"]),
                          pad_new(pr["bv"]), (mask,), pt_l, group)
        oc = _attn_sample("c", q3("cq"), head_pages(cache_c_k), head_pages(cache_c_v), pad_new(pr["ck"]),
                          pad_new(pr["cv"]), (cum_k, cum_q), pt_l, group)
        flat = lambda o: o.reshape(ms, -1).astype(BF16)
        h2 = _outproj(flat(oa), flat(ob), flat(oc), wts["w_out"], h1, g[3], ms)
        xs, _ = _ffn(h2, g[4], g[5], g[5], *wts["ffn2"], ms, tf)
        rows_s.append(_rows_out(pr, (dbs, dseq)))

    stack = lambda rows: tuple(jnp.stack([r[i] for r in rows]) for i in range(8))
    return (xp.reshape(bsz, seq, d), xs.reshape(dbs, dseq, d)) + stack(rows_p) + stack(rows_s)
```

```python
import functools
import math

import jax
import jax.numpy as jnp
from jax import lax
from jax.experimental import pallas as pl
from jax.experimental.pallas import tpu as pltpu

F32 = jnp.float32
BF16 = jnp.bfloat16
I32 = jnp.int32

D_MODEL = 2048
HEAD_DIM = 128
A_HEADS = 6
A_QK = 64
B_HEADS = 5
C_HEADS = 5
IDX_HEADS = 16
IDX_DIM = 64
TOPK_MAX = 256
PAGE_SIZE = 128
ROPE_THETA = 10000.0
NORM_EPS = 1e-6
LANES = 128
SUBLANES = 8
MXU_DIM = 256
NEG = -1e30
LOG2E = math.log2(math.e)
INT_MIN = -2 ** 31
VMEM_LIMIT = 56 * 1024 * 1024

W_A = A_HEADS * HEAD_DIM
W_B = B_HEADS * HEAD_DIM
W_C = C_HEADS * HEAD_DIM
W_IQ = IDX_HEADS * IDX_DIM
MISC_IW = IDX_DIM
MISC_CF = IDX_DIM + IDX_HEADS


def _cparams(sem):
    return pltpu.CompilerParams(dimension_semantics=sem, vmem_limit_bytes=VMEM_LIMIT)


def _rms(y, g):
    return y * lax.rsqrt(jnp.mean(y * y, axis=-1, keepdims=True) + NORM_EPS) * g


def _dot(a, b):
    return jnp.dot(a, b, preferred_element_type=F32)


def _dot_nt(a, b):
    return lax.dot_general(a, b, (((1,), (1,)), ((), ())), preferred_element_type=F32)


def _ffn_kernel(x_ref, gpre_ref, gpost_ref, gnext_ref, wg_ref, wu_ref, wd_ref,
                h_ref, hn_ref, xn_sc, acc_sc):
    j = pl.program_id(1)

    @pl.when(j == 0)
    def _():
        xn_sc[...] = _rms(x_ref[...], gpre_ref[...]).astype(BF16)
        acc_sc[...] = jnp.zeros_like(acc_sc)

    xn = xn_sc[...]
    g = _dot(xn, wg_ref[...])
    u = _dot(xn, wu_ref[...])
    a = (g * jax.nn.sigmoid(g) * u).astype(BF16)
    acc_sc[...] += _dot(a, wd_ref[...])

    @pl.when(j == pl.num_programs(1) - 1)
    def _():
        h = x_ref[...] + 0.5 * _rms(acc_sc[...], gpost_ref[...])
        h_ref[...] = h
        hn_ref[...] = _rms(h, gnext_ref[...]).astype(BF16)


def _ffn(x, gpre, gpost, gnext, wg, wu, wd, layer, tm, tf):
    m, d = x.shape
    ff = wg.shape[2]
    row = lambda i, j: (i, 0)
    gspec = pl.BlockSpec((1, d), lambda i, j: (0, 0))
    return pl.pallas_call(
        _ffn_kernel,
        grid=(m // tm, ff // tf),
        in_specs=[pl.BlockSpec((tm, d), row), gspec, gspec, gspec,
                  pl.BlockSpec((None, d, tf), lambda i, j: (layer, 0, j)),
                  pl.BlockSpec((None, d, tf), lambda i, j: (layer, 0, j)),
                  pl.BlockSpec((None, tf, d), lambda i, j: (layer, j, 0))],
        out_specs=[pl.BlockSpec((tm, d), row), pl.BlockSpec((tm, d), row)],
        out_shape=[jax.ShapeDtypeStruct((m, d), F32), jax.ShapeDtypeStruct((m, d), BF16)],
        scratch_shapes=[pltpu.VMEM((tm, d), BF16), pltpu.VMEM((tm, d), F32)],
        compiler_params=_cparams(("parallel", "arbitrary")),
        name="ffn",
    )(x, gpre, gpost, gnext, wg, wu, wd)


def _rope64(y, cos, sin_lo, sin_hi):
    return y * cos + pltpu.roll(y, 96, 1) * sin_lo + pltpu.roll(y, 32, 1) * sin_hi


def _rope128(y, cos, sin_signed):
    return y * cos + pltpu.roll(y, 64, 1) * sin_signed


def _proj_kernel(x_ref, w_ref, tab_ref, *refs, pieces, n_prev):
    out_refs = refs[n_prev:]
    y = _dot_nt(x_ref[...], w_ref[...])
    o = 0
    col = 0
    for width, kind, scale, f32_layout, emit_bf16 in pieces:
        for c in range(width // LANES):
            yc = y[:, col + c * LANES:col + (c + 1) * LANES]
            if kind == "rope64":
                yc = _rope64(yc, tab_ref[0], tab_ref[1], tab_ref[2])
            elif kind == "rope128":
                yc = _rope128(yc, tab_ref[3], tab_ref[4])
            if scale != 1.0:
                yc = yc * scale
            k = o
            if f32_layout == "rows":
                out_refs[k][:, c * LANES:(c + 1) * LANES] = yc
            elif f32_layout == "heads":
                out_refs[k][0, c] = yc
            elif f32_layout == "cols":
                out_refs[k][0, c * LANES:(c + 1) * LANES, :] = yc.T
            if f32_layout is not None:
                k += 1
            if emit_bf16:
                out_refs[k][:, c * LANES:(c + 1) * LANES] = yc.astype(BF16)
        o += int(f32_layout is not None) + int(emit_bf16)
        col += width


def _proj(hn, w_t, tab, pieces, tm, tab_blocks, tiles_per_batch, layer=0, depth=1, prev=()):
    m, d = hn.shape
    n = w_t.shape[0]
    bsz = m // (tm * tiles_per_batch)
    t = tm * tiles_per_batch
    rows = lambda i: (i, 0)
    batch_tile = lambda i: (layer, i // tiles_per_batch, 0, i % tiles_per_batch)
    out_specs, out_shape, stacked = [], [], []
    for width, _, _, f32_layout, emit_bf16 in pieces:
        if f32_layout == "rows":
            out_specs.append(pl.BlockSpec((tm, width), rows))
            out_shape.append(jax.ShapeDtypeStruct((m, width), F32))
        elif f32_layout == "heads":
            stacked.append(len(out_specs))
            out_specs.append(pl.BlockSpec((None, 1, width // LANES, tm, LANES), lambda i: batch_tile(i) + (0,)))
            out_shape.append(jax.ShapeDtypeStruct((depth, bsz, width // LANES, t, LANES), F32))
        elif f32_layout == "cols":
            stacked.append(len(out_specs))
            out_specs.append(pl.BlockSpec((None, 1, width, tm), batch_tile))
            out_shape.append(jax.ShapeDtypeStruct((depth, bsz, width, t), F32))
        if emit_bf16:
            out_specs.append(pl.BlockSpec((tm, width), rows))
            out_shape.append(jax.ShapeDtypeStruct((m, width), BF16))
    assert len(prev) in (0, len(stacked))
    return pl.pallas_call(
        functools.partial(_proj_kernel, pieces=pieces, n_prev=len(prev)),
        grid=(m // tm,),
        in_specs=[pl.BlockSpec((tm, d), rows),
                  pl.BlockSpec((n, d), lambda i: (0, 0)),
                  pl.BlockSpec((5, tm, LANES), lambda i: (0, i % tab_blocks, 0))]
        + [pl.BlockSpec(memory_space=pl.ANY)] * len(prev),
        out_specs=out_specs,
        out_shape=out_shape,
        input_output_aliases={3 + k: stacked[k] for k in range(len(prev))},
        compiler_params=_cparams(("parallel",)),
        name="proj",
    )(hn, w_t, tab, *prev)


def _misc_kernel(x_ref, w_ref, tab_ref, bf_ref, misc_ref, misc_t_ref, ikk_ref):
    y = _dot_nt(x_ref[...], w_ref[...])
    lane = lax.broadcasted_iota(I32, y.shape, 1)
    ik = _rope64(y, tab_ref[0], tab_ref[1], tab_ref[2])
    z = y + bf_ref[...]
    logf = jnp.minimum(z, 0.0) - jnp.log1p(jnp.exp(-jnp.abs(z)))
    iw = y * (IDX_HEADS * IDX_DIM) ** -0.5
    misc = jnp.where(lane < MISC_IW, ik, jnp.where(lane < MISC_CF, iw, logf))
    misc_ref[...] = misc
    misc_t_ref[...] = misc.T
    ikk_ref[...] = jnp.where(lane < IDX_DIM, ik, pltpu.roll(ik, IDX_DIM, 1)).astype(BF16)


def _misc(hn, w_t, tab, bf_pad, tm, tab_blocks):
    m, d = hn.shape
    return pl.pallas_call(
        _misc_kernel,
        grid=(m // tm,),
        in_specs=[pl.BlockSpec((tm, d), lambda i: (i, 0)),
                  pl.BlockSpec((LANES, d), lambda i: (0, 0)),
                  pl.BlockSpec((5, tm, LANES), lambda i: (0, i % tab_blocks, 0)),
                  pl.BlockSpec((1, LANES), lambda i: (0, 0))],
        out_specs=[pl.BlockSpec((tm, LANES), lambda i: (i, 0)),
                   pl.BlockSpec((LANES, tm), lambda i: (0, i)),
                   pl.BlockSpec((tm, LANES), lambda i: (i, 0))],
        out_shape=[jax.ShapeDtypeStruct((m, LANES), F32),
                   jax.ShapeDtypeStruct((LANES, m), F32),
                   jax.ShapeDtypeStruct((m, LANES), BF16)],
        compiler_params=_cparams(("parallel",)),
        name="proj_misc",
    )(hn, w_t, tab, bf_pad)


def _outproj_kernel(oa_ref, ob_ref, oc_ref, w_ref, h_ref, g_ref, out_ref):
    m = _dot(oa_ref[...], w_ref[0:W_A, :])
    m += _dot(ob_ref[...], w_ref[W_A:W_A + W_B, :])
    m += _dot(oc_ref[...], w_ref[W_A + W_B:, :])
    out_ref[...] = h_ref[...] + _rms(m, g_ref[...])


def _outproj(oa, ob, oc, w, layer, h, g, tm):
    m, d = h.shape
    row = lambda i: (i, 0)
    return pl.pallas_call(
        _outproj_kernel,
        grid=(m // tm,),
        in_specs=[pl.BlockSpec((tm, W_A), row), pl.BlockSpec((tm, W_B), row),
                  pl.BlockSpec((tm, W_C), row),
                  pl.BlockSpec((None, d, d), lambda i: (layer, 0, 0)),
                  pl.BlockSpec((tm, d), row),
                  pl.BlockSpec((1, d), lambda i: (0, 0))],
        out_specs=pl.BlockSpec((tm, d), row),
        out_shape=jax.ShapeDtypeStruct((m, d), F32),
        compiler_params=_cparams(("parallel",)),
        name="outproj",
    )(oa, ob, oc, w, h, g)


def _split3(x):
    hi = x.astype(BF16)
    r = x - hi.astype(F32)
    mid = r.astype(BF16)
    lo = (r - mid.astype(F32)).astype(BF16)
    return hi, mid, lo


def _tri(n, kind):
    r = lax.broadcasted_iota(I32, (n, n), 0)
    c = lax.broadcasted_iota(I32, (n, n), 1)
    m = {"lower_incl": c <= r, "lower_strict": c < r, "upper_incl": r <= c, "upper_strict": r < c}[kind]
    return jnp.where(m, 1.0, 0.0).astype(BF16)


def _prefix_rows(x, tri):
    hi, mid, lo = _split3(x)
    return _dot(tri, hi) + _dot(tri, mid) + _dot(tri, lo)


def _prefix_lanes(x, tri):
    hi, mid, lo = _split3(x)
    return _dot(hi, tri) + _dot(mid, tri) + _dot(lo, tri)


def _cum_prompt_kernel(misc_ref, col_ref, row_ref, *, blk):
    t = misc_ref.shape[0]
    tri = _tri(blk, "lower_incl")
    carry = jnp.zeros((1, LANES), F32)
    for b in range(t // blk):
        c = _prefix_rows(misc_ref[b * blk:(b + 1) * blk, :], tri) + carry
        col_ref[b * blk:(b + 1) * blk, :] = c
        row_ref[0, b] = c.T
        carry = c[blk - 1:blk, :]


def _cum_prompt(misc, bsz, t, blk):
    return pl.pallas_call(
        functools.partial(_cum_prompt_kernel, blk=blk),
        grid=(bsz,),
        in_specs=[pl.BlockSpec((t, LANES), lambda b: (b, 0))],
        out_specs=[pl.BlockSpec((t, LANES), lambda b: (b, 0)),
                   pl.BlockSpec((1, t // blk, LANES, blk), lambda b: (b, 0, 0, 0))],
        out_shape=[jax.ShapeDtypeStruct((bsz * t, LANES), F32),
                   jax.ShapeDtypeStruct((bsz, t // blk, LANES, blk), F32)],
        compiler_params=_cparams(("parallel",)),
        name="cum_prompt",
    )(misc)


def _sortable(score, admissible=None):
    score = jnp.where(score == 0.0, 0.0, score)
    bits = lax.bitcast_convert_type(score, I32)
    key = jnp.where(bits < 0, bits ^ 0x7FFFFFFF, bits)
    return key if admissible is None else jnp.where(admissible, key, INT_MIN)


def _select_bias(keys, thr, rank, need):
    tie_bias = jnp.where(rank < need, 0.0, NEG)
    bias = jnp.where(keys > thr, 0.0, jnp.where(keys == thr, tie_bias, NEG))
    return jnp.where(keys == INT_MIN, NEG, bias)


def _threshold(count_ge, shape, topk):
    def search(i, thr):
        cand = thr + jnp.left_shift(jnp.int32(1), 31 - i)
        return jnp.where(count_ge(cand) >= topk, cand, thr)
    return lax.fori_loop(0, 32, search, jnp.full(shape, INT_MIN, I32))


def _idx_prompt_kernel(iq_ref, ikk_ref, misc_ref, mask_ref, iqm_sc, keys_sc, *, tq, tk, topk):
    qi = pl.program_id(1)
    nkb = mask_ref.shape[2]
    lane_q = lax.broadcasted_iota(I32, (tq, LANES), 1)
    for h in range(IDX_HEADS):
        blk = iq_ref[:, (h // 2) * LANES:(h // 2 + 1) * LANES]
        keep = (lane_q >= IDX_DIM) if (h % 2) else (lane_q < IDX_DIM)
        iqm_sc[h * tq:(h + 1) * tq, :] = jnp.where(keep, blk, jnp.zeros_like(blk))
    w_t = misc_ref[...].T
    krow = lax.broadcasted_iota(I32, (tk, tq), 0)
    qcol = lax.broadcasted_iota(I32, (tk, tq), 1) + qi * tq

    def score_block(kb, _):
        kblk = ikk_ref[pl.ds(pl.multiple_of(kb * tk, tk), tk), :]
        score = jnp.zeros((tk, tq), F32)
        for h in range(IDX_HEADS):
            s = _dot_nt(kblk, iqm_sc[h * tq:(h + 1) * tq, :])
            score += w_t[MISC_IW + h:MISC_IW + h + 1, :] * jnp.maximum(s, 0.0)
        keys_sc[kb] = _sortable(score, krow + kb * tk <= qcol)
        return 0

    lax.fori_loop(0, qi + 1, score_block, 0)

    def count(pred):
        def body(kb, cnt):
            hit = jnp.where(pred(keys_sc[kb]), 1.0, 0.0)
            return cnt + jnp.sum(hit.reshape(tk // SUBLANES, SUBLANES, tq), axis=0)
        cnt = lax.fori_loop(0, qi + 1, body, jnp.zeros((SUBLANES, tq), F32))
        return jnp.sum(cnt, axis=0, keepdims=True)

    thr = _threshold(lambda cand: count(lambda k: k >= cand), (1, tq), topk)
    need = topk - count(lambda k: k > thr)
    tri = _tri(tk, "lower_strict")

    def emit(kb, seen):
        keys = keys_sc[kb]
        tie_f = jnp.where(keys == thr, 1.0, 0.0)
        rank = _dot(tri, tie_f.astype(BF16)) + seen
        mask_ref[0, 0, kb] = _select_bias(keys, thr, rank, need).T
        return seen + jnp.sum(tie_f, axis=0, keepdims=True)

    lax.fori_loop(0, qi + 1, emit, jnp.zeros((1, tq), F32))

    def fill(kb, _):
        mask_ref[0, 0, kb] = jnp.full((tq, tk), NEG, F32)
        return 0

    lax.fori_loop(qi + 1, nkb, fill, 0)


def _idx_prompt(iq_bf, ikk_bf, misc, bsz, t, tq, tk, topk):
    nq, nkb = t // tq, t // tk
    return pl.pallas_call(
        functools.partial(_idx_prompt_kernel, tq=tq, tk=tk, topk=topk),
        grid=(bsz, nq),
        in_specs=[pl.BlockSpec((tq, W_IQ), lambda b, q: (b * nq + q, 0)),
                  pl.BlockSpec((t, LANES), lambda b, q: (b, 0)),
                  pl.BlockSpec((tq, LANES), lambda b, q: (b * nq + q, 0))],
        out_specs=pl.BlockSpec((1, 1, nkb, tq, tk), lambda b, q: (b, q, 0, 0, 0)),
        out_shape=jax.ShapeDtypeStruct((bsz, nq, nkb, tq, tk), F32),
        scratch_shapes=[pltpu.VMEM((IDX_HEADS * tq, LANES), BF16),
                        pltpu.VMEM((nkb, tk, tq), I32)],
        compiler_params=_cparams(("parallel", "arbitrary")),
        name="idx_prompt",
    )(iq_bf, ikk_bf, misc)


def _lambda(lam_ref, lam_init):
    lq1, lk1, lq2, lk2 = lam_ref[0:1, :], lam_ref[1:2, :], lam_ref[2:3, :], lam_ref[3:4, :]
    return (jnp.exp(jnp.sum(lq1 * lk1, axis=-1, keepdims=True))
            - jnp.exp(jnp.sum(lq2 * lk2, axis=-1, keepdims=True)) + lam_init)


def _attn_prompt_kernel(*refs, mode, tq, tk, lam_init):
    if mode == "a":
        q_ref, k_ref, v_ref, lam_ref, gain_ref, o_ref = refs
    elif mode == "b":
        q_ref, k_ref, v_ref, mask_ref, o_ref = refs
    else:
        q_ref, k_ref, v_ref, cq_ref, ck_ref, o_ref = refs
    qi = pl.program_id(1)
    h = pl.program_id(2)
    q = q_ref[...]
    if mode == "a":
        lane = lax.broadcasted_iota(I32, q.shape, 1)
        zero = jnp.zeros_like(q)
        qs = jnp.concatenate([jnp.where(lane < A_QK, q, zero), jnp.where(lane >= A_QK, q, zero)], axis=0)
        scale = LOG2E
    else:
        qs = q
        scale = HEAD_DIM ** -0.5 * LOG2E
    rows = qs.shape[0]
    if mode == "c":
        lane = lax.broadcasted_iota(I32, (tq, LANES), 1)
        cum_q = jnp.sum(jnp.where(lane == MISC_CF + h, cq_ref[...], 0.0), axis=-1, keepdims=True) * LOG2E

    def step(kb, carry, diagonal):
        m, l, acc = carry
        off = pl.multiple_of(kb * tk, tk)
        s = _dot_nt(qs, k_ref[pl.ds(off, tk), :]) * scale
        if mode == "b":
            s = s + mask_ref[0, 0, kb]
        if mode == "c":
            s = s + (cum_q - ck_ref[0, kb, pl.ds(h, 1), :] * LOG2E)
        if diagonal:
            r = lax.broadcasted_iota(I32, (rows, tk), 0)
            c = lax.broadcasted_iota(I32, (rows, tk), 1)
            if rows > tq:
                r = jnp.where(r >= tq, r - tq, r)
            s = jnp.where(c <= r, s, NEG)
        m_new = jnp.maximum(m, jnp.max(s, axis=-1, keepdims=True))
        alpha = jnp.exp2(m - m_new)
        p = jnp.exp2(s - m_new)
        l = alpha * l + jnp.sum(p, axis=-1, keepdims=True)
        acc = alpha * acc + _dot(p.astype(BF16), v_ref[pl.ds(off, tk), :])
        return m_new, l, acc

    carry = (jnp.full((rows, 1), NEG, F32), jnp.zeros((rows, 1), F32), jnp.zeros((rows, LANES), F32))
    carry = lax.fori_loop(0, qi, lambda kb, c: step(kb, c, False), carry)
    _, l, acc = step(qi, carry, True)
    o = acc / l
    if mode == "a":
        lam = _lambda(lam_ref, lam_init)
        o = o[:tq] - lam * o[tq:]
        o = _rms(o, gain_ref[...]) * (1.0 - lam_init)
    o_ref[...] = o.astype(BF16)


def _attn_prompt(mode, q, k, v, extra, bsz, t, tq, lam_init=0.0):
    heads = q.shape[1] // LANES
    nq = t // tq
    tk = tq
    qspec = pl.BlockSpec((tq, LANES), lambda b, i, h: (b * nq + i, h))
    kvspec = pl.BlockSpec((t, LANES), lambda b, i, h: (b, h))
    if mode == "a":
        especs = [pl.BlockSpec((4, A_QK), lambda b, i, h: (0, 0)),
                  pl.BlockSpec((1, LANES), lambda b, i, h: (0, 0))]
    elif mode == "b":
        especs = [pl.BlockSpec((1, 1, t // tk, tq, tk), lambda b, i, h: (b, i, 0, 0, 0))]
    else:
        especs = [pl.BlockSpec((tq, LANES), lambda b, i, h: (b * nq + i, 0)),
                  pl.BlockSpec((1, t // tk, SUBLANES, tk), lambda b, i, h: (b, 0, MISC_CF // SUBLANES, 0))]
    return pl.pallas_call(
        functools.partial(_attn_prompt_kernel, mode=mode, tq=tq, tk=tk, lam_init=lam_init),
        grid=(bsz, nq, heads),
        in_specs=[qspec, kvspec, kvspec] + especs,
        out_specs=qspec,
        out_shape=jax.ShapeDtypeStruct(q.shape, BF16),
        compiler_params=_cparams(("parallel", "parallel", "arbitrary")),
        name="attn_prompt_" + mode,
    )(q, k, v, *extra)


def _page_specs(shape, group):
    def spec(g):
        return pl.BlockSpec((1,) + shape, lambda b, s, pt: (pt[b, s * group + g],) + (0,) * len(shape))
    return [spec(g) for g in range(group)]


def _exclusive_page_scan(tot, before_sc):
    seen = jnp.zeros((SUBLANES, 1), F32)
    for p in range(tot.shape[0]):
        before_sc[p] = seen
        seen = seen + tot[p]
    return seen


def _cum_sample_kernel(pt_ref, *refs, group):
    del pt_ref
    page_refs = refs[:group]
    new_ref, cumk_ref, cumq_ref, lf_sc, before_sc = refs[group:]
    s = pl.program_id(1)
    npg = lf_sc.shape[0]
    for g in range(group):
        lf_sc[s * group + g] = page_refs[g][0]

    @pl.when(s == pl.num_programs(1) - 1)
    def _():
        tri = _tri(LANES, "upper_incl")
        c = _prefix_lanes(lf_sc[...].reshape(npg * SUBLANES, LANES), tri).reshape(npg, SUBLANES, LANES)
        total = _exclusive_page_scan(c[:, :, LANES - 1:LANES], before_sc)
        cumk_ref[0, 0:npg] = c + before_sc[...]
        pad = jnp.zeros((LANES - SUBLANES, LANES), F32)
        lf_new = jnp.concatenate([new_ref[0], pad], axis=0).T[MISC_CF:MISC_CF + SUBLANES, :]
        c_new = _prefix_lanes(lf_new, tri) + total
        cumk_ref[0, npg] = c_new
        cumq_ref[0] = jnp.concatenate([c_new, pad], axis=0).T[0:SUBLANES, :]


def _cum_sample(logf_pages, misc_new, page_table, group):
    bsz, npg = page_table.shape
    grid_spec = pltpu.PrefetchScalarGridSpec(
        num_scalar_prefetch=1,
        grid=(bsz, npg // group),
        in_specs=_page_specs((SUBLANES, PAGE_SIZE), group)
        + [pl.BlockSpec((1, SUBLANES, LANES), lambda b, s, pt: (b, 0, 0))],
        out_specs=[pl.BlockSpec((1, npg + 1, SUBLANES, PAGE_SIZE), lambda b, s, pt: (b, 0, 0, 0)),
                   pl.BlockSpec((1, SUBLANES, LANES), lambda b, s, pt: (b, 0, 0))],
        scratch_shapes=[pltpu.VMEM((npg, SUBLANES, PAGE_SIZE), F32),
                        pltpu.VMEM((npg, SUBLANES, 1), F32)],
    )
    return pl.pallas_call(
        functools.partial(_cum_sample_kernel, group=group),
        grid_spec=grid_spec,
        out_shape=[jax.ShapeDtypeStruct((bsz, npg + 1, SUBLANES, PAGE_SIZE), F32),
                   jax.ShapeDtypeStruct((bsz, SUBLANES, LANES), F32)],
        compiler_params=_cparams(("parallel", "arbitrary")),
        name="cum_sample",
    )(page_table, *([logf_pages] * group), misc_new)


def _idx_sample_kernel(pt_ref, *refs, group, topk):
    del pt_ref
    page_refs = refs[:group]
    iq_ref, iw_ref, iknew_ref, mask_ref, keys_sc, before_sc = refs[group:]
    s = pl.program_id(1)
    npg = keys_sc.shape[0] - 1
    iq = iq_ref[0]
    w = jnp.broadcast_to(iw_ref[0], (IDX_HEADS * SUBLANES, LANES))

    def scores(qk):
        rel = jnp.maximum(qk, 0.0) * w
        return jnp.sum(rel.reshape(IDX_HEADS, SUBLANES, LANES), axis=0)

    for g in range(group):
        keys_sc[s * group + g] = _sortable(scores(_dot(iq, page_refs[g][0].astype(BF16))))

    @pl.when(s == pl.num_programs(1) - 1)
    def _():
        tok = lax.broadcasted_iota(I32, (SUBLANES, LANES), 0)
        lane = lax.broadcasted_iota(I32, (SUBLANES, LANES), 1)
        keys_sc[npg] = _sortable(scores(_dot_nt(iq, iknew_ref[0].astype(BF16))), lane <= tok)
        keys = keys_sc[...]

        def count(pred):
            hit = jnp.sum(jnp.where(pred, 1.0, 0.0), axis=0)
            return jnp.sum(hit, axis=-1, keepdims=True)

        thr = _threshold(lambda cand: count(keys >= cand[None]), (SUBLANES, 1), topk)
        need = topk - count(keys > thr[None])
        tie_f = jnp.where(keys == thr[None], 1.0, 0.0)
        rank = _dot(tie_f.astype(BF16).reshape((npg + 1) * SUBLANES, LANES), _tri(LANES, "upper_strict"))
        _exclusive_page_scan(jnp.sum(tie_f, axis=-1, keepdims=True), before_sc)
        rank = rank.reshape(npg + 1, SUBLANES, LANES) + before_sc[...]
        mask_ref[0] = _select_bias(keys, thr[None], rank, need[None])


def _idx_sample(idx_pages, iq_rows, iw_rows, ik_new, page_table, group, topk):
    bsz, npg = page_table.shape
    grid_spec = pltpu.PrefetchScalarGridSpec(
        num_scalar_prefetch=1,
        grid=(bsz, npg // group),
        in_specs=_page_specs((IDX_DIM, PAGE_SIZE), group)
        + [pl.BlockSpec((1, IDX_HEADS * SUBLANES, IDX_DIM), lambda b, s, pt: (b, 0, 0)),
           pl.BlockSpec((1, IDX_HEADS * SUBLANES, 1), lambda b, s, pt: (b, 0, 0)),
           pl.BlockSpec((1, PAGE_SIZE, IDX_DIM), lambda b, s, pt: (b, 0, 0))],
        out_specs=pl.BlockSpec((1, npg + 1, SUBLANES, PAGE_SIZE), lambda b, s, pt: (b, 0, 0, 0)),
        scratch_shapes=[pltpu.VMEM((npg + 1, SUBLANES, PAGE_SIZE), I32),
                        pltpu.VMEM((npg + 1, SUBLANES, 1), F32)],
    )
    return pl.pallas_call(
        functools.partial(_idx_sample_kernel, group=group, topk=topk),
        grid_spec=grid_spec,
        out_shape=jax.ShapeDtypeStruct((bsz, npg + 1, SUBLANES, PAGE_SIZE), F32),
        compiler_params=_cparams(("parallel", "arbitrary")),
        name="idx_sample",
    )(page_table, *([idx_pages] * group), iq_rows, iw_rows, ik_new)


def _attn_sample_kernel(pt_ref, *refs, mode, group, npg, rows, lam_init):
    del pt_ref
    k_pages, v_pages = refs[:group], refs[group:2 * group]
    rest = refs[2 * group:]
    if mode == "a":
        q_ref, knew_ref, vnew_ref, lam_ref, gain_ref, o_ref, qbd_sc, m_sc, l_sc, acc_sc = rest
    elif mode == "b":
        q_ref, knew_ref, vnew_ref, mask_ref, o_ref, qbd_sc, m_sc, l_sc, acc_sc = rest
    else:
        q_ref, knew_ref, vnew_ref, cumk_ref, cumq_ref, o_ref, qbd_sc, m_sc, l_sc, acc_sc = rest
    s = pl.program_id(1)
    width = q_ref.shape[-1]
    map_width = A_QK if mode == "a" else HEAD_DIM
    heads = width // HEAD_DIM
    reps = rows // SUBLANES
    scale = None if mode == "a" else HEAD_DIM ** -0.5

    def bias(page, new):
        tok = lax.broadcasted_iota(I32, (rows, LANES), 0) % SUBLANES
        lane = lax.broadcasted_iota(I32, (rows, LANES), 1)
        causal = jnp.where(lane <= tok, 0.0, NEG)
        if mode == "a":
            return causal if new else None
        if mode == "b":
            return jnp.concatenate([mask_ref[0, page]] * reps, axis=0)
        ck = cumk_ref[0, page]
        cq = cumq_ref[0]
        parts = [cq[:, hh:hh + 1] - ck[hh:hh + 1, :] for hh in range(heads)]
        parts += [jnp.zeros((SUBLANES, LANES), F32)] * (reps - heads)
        b = jnp.concatenate(parts, axis=0)
        return b + causal if new else b

    def update(tiles):
        m_old = m_sc[...]
        m_new = m_old
        for sc, _ in tiles:
            m_new = jnp.maximum(m_new, jnp.max(sc, axis=-1, keepdims=True))
        alpha = jnp.exp(m_old - m_new)
        l = alpha * l_sc[...]
        acc = alpha * acc_sc[...]
        for sc, v in tiles:
            p = jnp.exp(sc - m_new)
            l = l + jnp.sum(p, axis=-1, keepdims=True)
            acc = acc + _dot(p.astype(BF16), v)
        m_sc[...] = m_new
        l_sc[...] = l
        acc_sc[...] = acc

    def by_head(ref):
        return jnp.concatenate([ref[0, hh] for hh in range(heads)], axis=1).astype(BF16)

    def tile(sc, v, page, new):
        if scale is not None:
            sc = sc * scale
        b = bias(page, new)
        if b is not None:
            sc = sc + b
        return sc, v

    def cached_tile(g):
        q = qbd_sc[...]
        if mode == "a":
            sc = _dot(q, k_pages[g][0].astype(BF16))
        else:
            sc = _dot_nt(q, by_head(k_pages[g]))
        return tile(sc, by_head(v_pages[g]), s * group + g, False)

    def new_tile():
        sc = _dot_nt(qbd_sc[...], knew_ref[0].astype(BF16))
        return tile(sc, vnew_ref[0].astype(BF16), npg, True)

    @pl.when(s == 0)
    def _():
        q = jnp.concatenate([q_ref[0]] * reps, axis=0)
        r = lax.broadcasted_iota(I32, q.shape, 0) // SUBLANES
        c = lax.broadcasted_iota(I32, q.shape, 1) // map_width
        qbd_sc[...] = jnp.where(r == c, q, 0.0).astype(BF16)
        m_sc[...] = jnp.full(m_sc.shape, NEG, F32)
        l_sc[...] = jnp.zeros_like(l_sc)
        acc_sc[...] = jnp.zeros_like(acc_sc)
        update([new_tile()])

    update([cached_tile(g) for g in range(group)])

    @pl.when(s == npg // group - 1)
    def _():
        on = acc_sc[...] / l_sc[...]
        lam = _lambda(lam_ref, lam_init) if mode == "a" else None
        for hh in range(heads):
            cols = slice(hh * HEAD_DIM, (hh + 1) * HEAD_DIM)
            if mode == "a":
                o = (on[2 * hh * SUBLANES:(2 * hh + 1) * SUBLANES, cols]
                     - lam * on[(2 * hh + 1) * SUBLANES:(2 * hh + 2) * SUBLANES, cols])
                o = _rms(o, gain_ref[...]) * (1.0 - lam_init)
            else:
                o = on[hh * SUBLANES:(hh + 1) * SUBLANES, cols]
            o_ref[0, :, cols] = o


def _attn_sample(mode, q, k_cache, v_cache, k_new, v_new, extra, page_table, group, lam_init=0.0):
    bsz, npg = page_table.shape
    width = q.shape[-1]
    rows = 2 * A_HEADS * SUBLANES if mode == "a" else 48
    batch3 = lambda b, s, pt: (b, 0, 0)
    new_spec = pl.BlockSpec((1, PAGE_SIZE, width), batch3)
    if mode == "a":
        especs = [pl.BlockSpec((4, A_QK), lambda b, s, pt: (0, 0)),
                  pl.BlockSpec((1, LANES), lambda b, s, pt: (0, 0))]
    elif mode == "b":
        especs = [pl.BlockSpec((1, npg + 1, SUBLANES, PAGE_SIZE), lambda b, s, pt: (b, 0, 0, 0))]
    else:
        especs = [pl.BlockSpec((1, npg + 1, SUBLANES, PAGE_SIZE), lambda b, s, pt: (b, 0, 0, 0)),
                  pl.BlockSpec((1, SUBLANES, LANES), batch3)]
    grid_spec = pltpu.PrefetchScalarGridSpec(
        num_scalar_prefetch=1,
        grid=(bsz, npg // group),
        in_specs=_page_specs(k_cache.shape[1:], group) + _page_specs(v_cache.shape[1:], group)
        + [pl.BlockSpec((1, SUBLANES, width), batch3), new_spec, new_spec] + especs,
        out_specs=pl.BlockSpec((1, SUBLANES, width), batch3),
        scratch_shapes=[pltpu.VMEM((rows, width), BF16), pltpu.VMEM((rows, 1), F32),
                        pltpu.VMEM((rows, 1), F32), pltpu.VMEM((rows, width), F32)],
    )
    return pl.pallas_call(
        functools.partial(_attn_sample_kernel, mode=mode, group=group, npg=npg, rows=rows,
                          lam_init=lam_init),
        grid_spec=grid_spec,
        out_shape=jax.ShapeDtypeStruct((bsz, SUBLANES, width), F32),
        compiler_params=_cparams(("parallel", "arbitrary")),
        name="attn_sample_" + mode,
    )(page_table, *([k_cache] * group), *([v_cache] * group), q, k_new, v_new, *extra)


_IN_SIZES = (W_A, W_A, W_A, W_B, W_B, W_B, W_IQ, IDX_DIM, IDX_HEADS, W_C, W_C, W_C, C_HEADS)
_IN_NAMES = ("aq", "ak", "av", "bq", "bk", "bv", "iq", "ik", "iw", "cq", "ck", "cv", "cf")

_PROJ_GROUPS = (
    (("aq", "ak", "iq"),
     ((W_A, "rope64", A_QK ** -0.5, None, True), (W_A, "rope64", 1.0, "cols", True),
      (W_IQ, "rope64", 1.0, None, True))),
    (("bq", "bk"),
     ((W_B, "rope128", 1.0, None, True), (W_B, "rope128", 1.0, "heads", True))),
    (("av", "bv", "cq", "ck", "cv"),
     ((W_A, "id", 1.0, "heads", True), (W_B, "id", 1.0, "heads", True), (W_C, "id", 1.0, None, True),
      (W_C, "id", 1.0, "heads", True), (W_C, "id", 1.0, "heads", True))),
)


def _row_major(pieces):
    return tuple((w, kind, scale, None if lay is None else "rows", bf) for w, kind, scale, lay, bf in pieces)


def _rope_tables(pos):
    lane = jnp.arange(LANES)
    posf = pos.astype(F32)[:, None]

    def cs(half):
        inv_freq = ROPE_THETA ** (-jnp.arange(half, dtype=F32) / half)
        ang = posf * inv_freq[lane % half][None, :]
        return jnp.cos(ang), jnp.sin(ang)

    c64, s64 = cs(IDX_DIM // 2)
    first64 = (lane % IDX_DIM) < IDX_DIM // 2
    c128, s128 = cs(HEAD_DIM // 2)
    return jnp.stack([c64, jnp.where(first64, -s64, 0.0), jnp.where(first64, 0.0, s64),
                      c128, jnp.where(lane < HEAD_DIM // 2, -s128, s128)]).astype(F32)


def _ffn_weights(gate, up, down, ff_pad):
    pad = ff_pad - gate.shape[-1]
    cast_pad = lambda w, axis: jnp.pad(w.astype(BF16), [(0, pad if a == axis else 0) for a in range(3)])
    return cast_pad(gate, 2), cast_pad(up, 2), cast_pad(down, 1)


def _layer_weights(l, w_in, b_forget, lam_q1, lam_k1, lam_q2, lam_k2, subln_gain, norm_gains):
    offs = {}
    o = 0
    for name, size in zip(_IN_NAMES, _IN_SIZES):
        offs[name] = (o, o + size)
        o += size
    w_t = jnp.transpose(w_in, (2, 0, 1))
    cols = lambda names: jnp.concatenate([w_t[offs[n][0]:offs[n][1], l, :] for n in names], axis=0).astype(BF16)
    misc_t = cols(("ik", "iw", "cf"))
    return dict(
        groups=[cols(names) for names, _ in _PROJ_GROUPS],
        misc=jnp.pad(misc_t, ((0, LANES - misc_t.shape[0]), (0, 0))),
        bf=jnp.zeros((1, LANES), F32).at[0, MISC_CF:MISC_CF + C_HEADS].set(b_forget[l].astype(F32)),
        lam=jnp.stack([lam_q1[l], lam_k1[l], lam_q2[l], lam_k2[l]]).astype(F32),
        gain=subln_gain[l].astype(F32)[None, :],
        g=[norm_gains[l, i].astype(F32)[None, :] for i in range(6)],
    )


def _project(hn, wts, tab, tm, tab_blocks, native, layer=0, depth=1, prev=None):
    out = {}
    for (names, pieces), w in zip(_PROJ_GROUPS, wts["groups"]):
        if native:
            stacked = [prev[n] for n, p in zip(names, pieces) if p[3] is not None] if prev else ()
            res = _proj(hn, w, tab, pieces, tm, tab_blocks, tab_blocks, layer, depth, stacked)
        else:
            pieces = _row_major(pieces)
            res = _proj(hn, w, tab, pieces, tm, tab_blocks, 1)
        k = 0
        for name, (_, _, _, f32_layout, emit_bf16) in zip(names, pieces):
            if f32_layout is not None:
                out[name] = res[k]
                k += 1
            if emit_bf16:
                out[name + "_bf"] = res[k]
                k += 1
    out["misc"], out["misc_t"], out["ikk_bf"] = _misc(hn, wts["misc"], tab, wts["bf"], tm, tab_blocks)
    return out


def _rows_out_native(pr, misc_ts, bsz, seq):
    depth = len(misc_ts)
    heads = lambda a: jnp.transpose(a, (0, 1, 3, 2, 4))
    misc_t = jnp.stack(misc_ts)
    lanes = lambda lo, n: jnp.transpose(misc_t[:, lo:lo + n].reshape(depth, n, bsz, seq), (0, 2, 3, 1))
    ak = jnp.transpose(pr["ak"].reshape(depth, bsz, A_HEADS, 2, A_QK, seq), (0, 1, 5, 2, 3, 4))
    return (ak, heads(pr["av"]), heads(pr["bk"]), heads(pr["bv"]), lanes(0, IDX_DIM),
            heads(pr["ck"]), heads(pr["cv"]), lanes(MISC_CF, C_HEADS))


def _rows_out(pr, lead):
    misc = pr["misc"]
    return (pr["ak"].reshape(lead + (A_HEADS, 2, A_QK)), pr["av"].reshape(lead + (A_HEADS, HEAD_DIM)),
            pr["bk"].reshape(lead + (B_HEADS, HEAD_DIM)), pr["bv"].reshape(lead + (B_HEADS, HEAD_DIM)),
            misc[:, :IDX_DIM].reshape(lead + (IDX_DIM,)),
            pr["ck"].reshape(lead + (C_HEADS, HEAD_DIM)), pr["cv"].reshape(lead + (C_HEADS, HEAD_DIM)),
            misc[:, MISC_CF:MISC_CF + C_HEADS].reshape(lead + (C_HEADS,)))


def kernel(x_prompt, x_sample, cache_a_k, cache_a_v, cache_b_k, cache_b_v, cache_b_idx,
           cache_c_k, cache_c_v, cache_c_logf, page_table, w_in, b_forget, lambda_q1, lambda_k1,
           lambda_q2, lambda_k2, subln_gain, w_out, ffn1_gate, ffn1_up, ffn1_down,
           ffn2_gate, ffn2_up, ffn2_down, norm_gains):
    bsz, seq, d = x_prompt.shape
    dbs, dseq, _ = x_sample.shape
    depth = w_in.shape[0]
    n_pool = cache_a_k.shape[1]
    npg = page_table.shape[1]
    past_len = npg * PAGE_SIZE
    ff = ffn1_gate.shape[-1]
    ms = dbs * dseq
    tf = 2 * MXU_DIM
    ff_pad = -(-ff // tf) * tf
    tm_p, tm_proj, tq = 512, 256, min(512, seq)
    group = 16
    group_small = math.gcd(npg, 64)
    assert dseq == SUBLANES and seq % tm_p == 0 and seq % tq == 0 and npg % group == 0
    ffn1 = _ffn_weights(ffn1_gate, ffn1_up, ffn1_down, ff_pad)
    ffn2 = _ffn_weights(ffn2_gate, ffn2_up, ffn2_down, ff_pad)
    w_out_bf = w_out.astype(BF16)

    tab_p = _rope_tables(jnp.arange(seq))
    tab_s = _rope_tables(past_len + (jnp.arange(ms) % dseq))
    topk_p = min(TOPK_MAX, seq // 4)
    topk_s = min(TOPK_MAX, (past_len + dseq) // 4)

    xp = x_prompt.reshape(bsz * seq, d)
    xs = x_sample.reshape(ms, d)
    rows_s, misc_ts = [], []
    pr_p = None
    pad_new = lambda a: jnp.pad(a.reshape(dbs, dseq, -1), ((0, 0), (0, PAGE_SIZE - dseq), (0, 0)))
    for l in range(depth):
        wts = _layer_weights(l, w_in, b_forget, lambda_q1, lambda_k1, lambda_q2, lambda_k2, subln_gain,
                             norm_gains)
        g = wts["g"]
        lam_init = 0.8 - 0.6 * math.exp(-0.3 * l)

        h1, hn = _ffn(xp, g[0], g[1], g[2], *ffn1, l, tm_p, tf)
        pr_p = pr = _project(hn, wts, tab_p, tm_proj, seq // tm_proj, True, l, depth, pr_p)
        misc_ts.append(pr["misc_t"])
        cum_col, cum_row = _cum_prompt(pr["misc"], bsz, seq, tq)
        mask = _idx_prompt(pr["iq_bf"], pr["ikk_bf"], pr["misc"], bsz, seq, tq, tq, topk_p)
        oa = _attn_prompt("a", pr["aq_bf"], pr["ak_bf"], pr["av_bf"], (wts["lam"], wts["gain"]),
                          bsz, seq, tq, lam_init)
        ob = _attn_prompt("b", pr["bq_bf"], pr["bk_bf"], pr["bv_bf"], (mask,), bsz, seq, tq)
        oc = _attn_prompt("c", pr["cq_bf"], pr["ck_bf"], pr["cv_bf"], (cum_col, cum_row), bsz, seq, tq)
        h2 = _outproj(oa, ob, oc, w_out_bf, l, h1, g[3], tm_p)
        xp, _ = _ffn(h2, g[4], g[5], g[5], *ffn2, l, tm_p, tf)

        h1, hn = _ffn(xs, g[0], g[1], g[2], *ffn1, l, ms, tf)
        pr = _project(hn, wts, tab_s, ms, 1, False)
        misc3 = pr["misc"].reshape(dbs, dseq, LANES)
        logf_pages = jnp.pad(jnp.swapaxes(cache_c_logf[l], 1, 2).astype(F32),
                             ((0, 0), (0, SUBLANES - C_HEADS), (0, 0)))
        cum_k, cum_q = _cum_sample(logf_pages, misc3, page_table, group_small)
        iq_rows = jnp.swapaxes(pr["iq_bf"].reshape(dbs, dseq, IDX_HEADS, IDX_DIM), 1, 2)
        iq_rows = iq_rows.reshape(dbs, IDX_HEADS * dseq, IDX_DIM)
        iw_rows = jnp.swapaxes(misc3[:, :, MISC_IW:MISC_CF], 1, 2).reshape(dbs, IDX_HEADS * dseq, 1)
        pt_l = page_table + l * n_pool
        pool = depth * n_pool
        kt_pages = lambda c, w: jnp.moveaxis(c.reshape(pool, PAGE_SIZE, w), 1, 2)
        head_pages = lambda c: jnp.swapaxes(c.reshape((pool,) + c.shape[2:]), 1, 2)
        mask = _idx_sample(kt_pages(cache_b_idx, IDX_DIM), iq_rows, iw_rows, pad_new(pr["misc"][:, :IDX_DIM]),
                           pt_l, group_small, topk_s)
        q3 = lambda name: pr[name + "_bf"].astype(F32).reshape(dbs, dseq, -1)
        oa = _attn_sample("a", q3("aq"), kt_pages(cache_a_k, W_A), head_pages(cache_a_v), pad_new(pr["ak"]),
                          pad_new(pr["av"]), (wts["lam"], wts["gain"]), pt_l, group, lam_init)
        ob = _attn_sample("b", q3("bq"), head_pages(cache_b_k), head_pages(cache_b_v), pad_new(pr["bk"]),
                          pad_new(pr["bv"]), (mask,), pt_l, group)
        oc = _attn_sample("c", q3("cq"), head_pages(cache_c_k), head_pages(cache_c_v), pad_new(pr["ck"]),
                          pad_new(pr["cv"]), (cum_k, cum_q), pt_l, group)
        flat = lambda o: o.reshape(ms, -1).astype(BF16)
        h2 = _outproj(flat(oa), flat(ob), flat(oc), w_out_bf, l, h1, g[3], ms)
        xs, _ = _ffn(h2, g[4], g[5], g[5], *ffn2, l, ms, tf)
        rows_s.append(_rows_out(pr, (dbs, dseq)))

    stack = lambda rows: tuple(jnp.stack([r[i] for r in rows]) for i in range(8))
    return ((xp.reshape(bsz, seq, d), xs.reshape(dbs, dseq, d))
            + _rows_out_native(pr_p, misc_ts, bsz, seq) + stack(rows_s))
```
